```python
import jax, jax.numpy as jnp
from jax import lax
import numpy as np

D_MODEL = 2048
BATCH = 4
SEQ = 2048
DEPTH = 4
DEC_BATCH = 16
DEC_SEQ = 16
PAST_LEN = 1024

CHUNK = 64
QBLOCK = 128
N_A = DEPTH // 2
N_B = DEPTH - N_A
H_A = 4
DK_A = D_MODEL // 2 // H_A
DV_A = D_MODEL // H_A
DK_TOT = H_A * DK_A
DV_TOT = H_A * DV_A
GATE_RANK = 16
GATE_NORMALIZER = 16.0
GLA_IN = 2 * DK_TOT + DV_TOT + GATE_RANK + DV_TOT
H_B = 16
HD_B = D_MODEL // H_B
KV_OUT = 2 * D_MODEL + H_B
N_EXPERTS = 32
TOP_K = 4
D_FF = D_MODEL
SWIGLU_LIMIT = 7.0
SWIGLU_ALPHA = 1.702
FORGET_BIAS = 2.0
EPS = 1e-6

kernel_name = 'yoco_gla_fox_moe_stream_step'


def rmsnorm(x, g):
    xf = x.astype(jnp.float32)
    y = xf * lax.rsqrt(jnp.mean(xf * xf, axis=-1, keepdims=True) + EPS) * g.astype(jnp.float32)
    return y.astype(x.dtype)


def modulate(xn, shift, scale):
    return xn * (1.0 + scale[:, None, :]) + shift[:, None, :]


def gla_scan(q, k, v, log_a, s0):
    b, h, t, dk = q.shape
    dv = v.shape[-1]
    L = min(CHUNK, t)
    n = t // L

    def blocks(z):
        return jnp.moveaxis(z.reshape(b, h, n, L, z.shape[-1]), 2, 0)

    causal = jnp.tril(jnp.ones((L, L), dtype=bool))

    def step(S, xs):
        qc, kc, vc, gc = xs
        G = jnp.cumsum(gc, axis=2)
        G_last = G[:, :, -1, :]
        q_dec = qc * jnp.exp(G)
        k_dec = kc * jnp.exp(-G)
        att = jnp.where(causal, jnp.einsum('bhtd,bhsd->bhts', q_dec, k_dec), 0.0)
        o = jnp.einsum('bhts,bhsv->bhtv', att, vc) + jnp.einsum('bhtd,bhdv->bhtv', q_dec, S)
        k_end = kc * jnp.exp(G_last[:, :, None, :] - G)
        S = S * jnp.exp(G_last)[..., None] + jnp.einsum('bhsd,bhsv->bhdv', k_end, vc)
        return S, o

    S, o = lax.scan(step, s0, (blocks(q), blocks(k), blocks(v), blocks(log_a)))
    return jnp.moveaxis(o, 0, 2).reshape(b, h, t, dv), S


def gla_mixer(h, s0, w_in, w_g2, b_g2, onorm, w_out):
    B, T, _ = h.shape
    proj = h @ w_in
    q, k, v, gl, r = jnp.split(proj, [DK_TOT, 2 * DK_TOT, 2 * DK_TOT + DV_TOT,
                                      2 * DK_TOT + DV_TOT + GATE_RANK], axis=-1)
    log_a = jax.nn.log_sigmoid((gl @ w_g2 + b_g2).astype(jnp.float32)) / GATE_NORMALIZER

    def heads(z, d):
        return z.reshape(B, T, H_A, d).transpose(0, 2, 1, 3).astype(jnp.float32)

    o, S = gla_scan(heads(q, DK_A) * (DK_A ** -0.5), heads(k, DK_A), heads(v, DV_A),
                    heads(log_a, DK_A), s0.astype(jnp.float32))
    o = rmsnorm(o.transpose(0, 2, 1, 3), onorm).reshape(B, T, DV_TOT)
    o = o * jax.nn.silu(r.astype(jnp.float32))
    return o.astype(h.dtype) @ w_out, S


def fox_attend(q, k_all, v_all, Cq, Ck, P):
    B, T = q.shape[:2]
    kpos = jnp.arange(k_all.shape[1])
    CkT = Ck.transpose(0, 2, 1)
    kf = k_all.astype(jnp.float32)
    vf = v_all.astype(jnp.float32)

    def block(q_blk, Cq_blk, start):
        qpos = P + start + jnp.arange(q_blk.shape[1])
        s = (jnp.einsum('bqhd,bkhd->bhqk', q_blk, kf) * (HD_B ** -0.5)
             + Cq_blk.transpose(0, 2, 1)[..., None] - CkT[:, :, None, :])
        s = jnp.where(kpos[None, :] <= qpos[:, None], s, -jnp.inf)
        p = jax.nn.softmax(s, axis=-1)
        return jnp.einsum('bhqk,bkhd->bqhd', p, vf)

    if T <= QBLOCK:
        return block(q, Cq, 0)
    nb = T // QBLOCK
    qb = q.reshape(B, nb, QBLOCK, H_B, HD_B).swapaxes(0, 1)
    Cb = Cq.reshape(B, nb, QBLOCK, H_B).swapaxes(0, 1)
    starts = jnp.arange(nb) * QBLOCK
    o = lax.map(lambda a: block(a[0], a[1], a[2]), (qb, Cb, starts))
    return o.swapaxes(0, 1).reshape(B, T, H_B, HD_B)


def moe(h, w_router, b_router, w_in, b_in, w_out, b_out):
    B, T, D = h.shape
    xf = h.reshape(B * T, D)
    logits = (xf @ w_router + b_router).astype(jnp.float32)
    top_v, top_i = lax.top_k(logits, TOP_K)
    gates = jnp.sum(jax.nn.one_hot(top_i, N_EXPERTS, dtype=jnp.float32)
                    * jax.nn.softmax(top_v, axis=-1)[..., None], axis=1)
    out = jnp.zeros((B * T, D), jnp.float32)
    for e in range(N_EXPERTS):
        gu = (xf @ w_in[e] + b_in[e]).astype(jnp.float32)
        g = jnp.minimum(gu[:, :D_FF], SWIGLU_LIMIT)
        u = jnp.clip(gu[:, D_FF:], -SWIGLU_LIMIT, SWIGLU_LIMIT)
        a = (g * jax.nn.sigmoid(SWIGLU_ALPHA * g) * (u + 1.0)).astype(h.dtype)
        out = out + gates[:, e:e + 1] * (a @ w_out[e] + b_out[e]).astype(jnp.float32)
    return out.astype(h.dtype).reshape(B, T, D)


def trunk(x, c, gla_s0, k_past, v_past, logf_past,
          w_mod, b_mod, norm_mix, norm_ffn,
          w_gla_in, w_gla_g2, b_gla_g2, gla_onorm, w_gla_out,
          w_kv_mod, b_kv_mod, norm_kv, w_kv, b_f, k_norm,
          w_fox_q, q_norm, w_fox_out,
          w_router, b_router, w_moe_in, b_moe_in, w_moe_out, b_moe_out):
    B, T, _ = x.shape
    P = k_past.shape[1]
    cs = jax.nn.silu(c.astype(jnp.float32))
    gla_states = []
    for l in range(DEPTH):
        if l == N_A:
            sh_kv, sc_kv = jnp.split(cs @ w_kv_mod + b_kv_mod, 2, axis=-1)
            hk = modulate(rmsnorm(x, norm_kv).astype(jnp.float32), sh_kv, sc_kv).astype(x.dtype)
            proj = hk @ w_kv
            k_new = rmsnorm(proj[..., :D_MODEL].reshape(B, T, H_B, HD_B), k_norm)
            v_new = proj[..., D_MODEL:2 * D_MODEL].reshape(B, T, H_B, HD_B)
            logf_new = jax.nn.log_sigmoid((proj[..., 2 * D_MODEL:] + b_f).astype(jnp.float32))
            k_all = jnp.concatenate([k_past, k_new], axis=1)
            v_all = jnp.concatenate([v_past, v_new], axis=1)
            C_all = jnp.cumsum(jnp.concatenate([logf_past.astype(jnp.float32), logf_new], axis=1), axis=1)
        mod = cs @ w_mod[l] + b_mod[l]
        sh1, sc1, g1, sh2, sc2, g2 = jnp.split(mod, 6, axis=-1)
        h = modulate(rmsnorm(x, norm_mix[l]).astype(jnp.float32), sh1, sc1).astype(x.dtype)
        if l < N_A:
            o, S = gla_mixer(h, gla_s0[l], w_gla_in[l], w_gla_g2[l], b_gla_g2[l],
                             gla_onorm[l], w_gla_out[l])
            gla_states.append(S)
        else:
            j = l - N_A
            q = rmsnorm((h @ w_fox_q[j]).reshape(B, T, H_B, HD_B).astype(jnp.float32), q_norm[j])
            o = fox_attend(q, k_all, v_all, C_all[:, P:], C_all, P)
            o = o.reshape(B, T, D_MODEL).astype(x.dtype) @ w_fox_out[j]
        x = (x + g1[:, None, :] * o).astype(x.dtype)
        h = modulate(rmsnorm(x, norm_ffn[l]).astype(jnp.float32), sh2, sc2).astype(x.dtype)
        y = moe(h, w_router[l], b_router[l], w_moe_in[l], b_moe_in[l], w_moe_out[l], b_moe_out[l])
        x = (x + g2[:, None, :] * y).astype(x.dtype)
    return x, jnp.stack(gla_states), k_new, v_new, logf_new


def setup_inputs(seed: int = 0) -> dict:
    key = jax.random.key(seed)
    ks = iter(jax.random.split(key, 48))

    def nrm(shape, scale):
        return jax.random.normal(next(ks), shape, jnp.float32) * scale

    D = D_MODEL
    return {
        'x_prompt': nrm((BATCH, SEQ, D), 1.0),
        'x_sample': nrm((DEC_BATCH, DEC_SEQ, D), 1.0),
        'c_prompt': nrm((BATCH, D), 1.0),
        'c_sample': nrm((DEC_BATCH, D), 1.0),
        'state_gla': nrm((N_A, DEC_BATCH, H_A, DK_A, DV_A), 0.3),
        'cache_k': nrm((DEC_BATCH, PAST_LEN, H_B, HD_B), 1.0),
        'cache_v': nrm((DEC_BATCH, PAST_LEN, H_B, HD_B), 1.0),
        'cache_logf': jax.nn.log_sigmoid(FORGET_BIAS + nrm((DEC_BATCH, PAST_LEN, H_B), 1.0)),
        'w_mod': nrm((DEPTH, D, 6 * D), 0.5 * D ** -0.5),
        'b_mod': nrm((DEPTH, 6 * D), 0.02),
        'norm_mix': 1.0 + nrm((DEPTH, D), 0.02),
        'norm_ffn': 1.0 + nrm((DEPTH, D), 0.02),
        'w_gla_in': nrm((N_A, D, GLA_IN), D ** -0.5),
        'w_gla_g2': nrm((N_A, GATE_RANK, DK_TOT), GATE_RANK ** -0.5),
        'b_gla_g2': nrm((N_A, DK_TOT), 0.1),
        'gla_onorm': 1.0 + nrm((N_A, DV_A), 0.02),
        'w_gla_out': nrm((N_A, DV_TOT, D), DV_TOT ** -0.5),
        'w_kv_mod': nrm((D, 2 * D), 0.5 * D ** -0.5),
        'b_kv_mod': nrm((2 * D,), 0.02),
        'norm_kv': 1.0 + nrm((D,), 0.02),
        'w_kv': nrm((D, KV_OUT), D ** -0.5),
        'b_f': FORGET_BIAS + nrm((H_B,), 0.1),
        'k_norm': 1.0 + nrm((HD_B,), 0.02),
        'w_fox_q': nrm((N_B, D, D), D ** -0.5),
        'q_norm': 1.0 + nrm((N_B, HD_B), 0.02),
        'w_fox_out': nrm((N_B, D, D), D ** -0.5),
        'w_router': nrm((DEPTH, D, N_EXPERTS), D ** -0.5),
        'b_router': nrm((DEPTH, N_EXPERTS), 0.01),
        'w_moe_in': nrm((DEPTH, N_EXPERTS, D, 2 * D_FF), D ** -0.5),
        'b_moe_in': nrm((DEPTH, N_EXPERTS, 2 * D_FF), 0.01),
        'w_moe_out': nrm((DEPTH, N_EXPERTS, D_FF, D), D_FF ** -0.5),
        'b_moe_out': nrm((DEPTH, N_EXPERTS, D), 0.01),
    }


def reference(x_prompt, x_sample, c_prompt, c_sample, state_gla, cache_k, cache_v, cache_logf,
              w_mod, b_mod, norm_mix, norm_ffn,
              w_gla_in, w_gla_g2, b_gla_g2, gla_onorm, w_gla_out,
              w_kv_mod, b_kv_mod, norm_kv, w_kv, b_f, k_norm,
              w_fox_q, q_norm, w_fox_out,
              w_router, b_router, w_moe_in, b_moe_in, w_moe_out, b_moe_out):
    weights = (w_mod, b_mod, norm_mix, norm_ffn,
               w_gla_in, w_gla_g2, b_gla_g2, gla_onorm, w_gla_out,
               w_kv_mod, b_kv_mod, norm_kv, w_kv, b_f, k_norm,
               w_fox_q, q_norm, w_fox_out,
               w_router, b_router, w_moe_in, b_moe_in, w_moe_out, b_moe_out)
    Bp = x_prompt.shape[0]
    s0_prompt = jnp.zeros((N_A, Bp, H_A, DK_A, DV_A), jnp.float32)
    empty_kv = jnp.zeros((Bp, 0, H_B, HD_B), x_prompt.dtype)
    empty_f = jnp.zeros((Bp, 0, H_B), jnp.float32)
    y_prompt, gla_p, k_p, v_p, f_p = trunk(x_prompt, c_prompt, s0_prompt, empty_kv, empty_kv, empty_f, *weights)
    y_sample, gla_s, k_s, v_s, f_s = trunk(x_sample, c_sample, state_gla, cache_k, cache_v, cache_logf, *weights)
    return (y_prompt, y_sample, gla_p, gla_s, k_p, v_p, f_p, k_s, v_s, f_s)
```

```python
import collections
import functools

import jax
import jax.numpy as jnp
from jax import lax
from jax.experimental import pallas as pl
from jax.experimental.pallas import tpu as pltpu

F32 = jnp.float32
BF16 = jnp.bfloat16
I32 = jnp.int32

D_MODEL = 2048
DEPTH = 4
N_A = 2
H_A = 4
DK_A = 256
DV_A = 512
DK_TOT = H_A * DK_A
DV_TOT = H_A * DV_A
GATE_RANK = 16
GATE_NORMALIZER = 16.0
H_B = 16
HD_B = 128
N_EXPERTS = 32
TOP_K = 4
D_FF = D_MODEL
SWIGLU_LIMIT = 7.0
SWIGLU_ALPHA = 1.702
EPS = 1e-6
CHUNK = 64

ROW_TILE = 256
LANES = 128
LANE_GROUPS = D_MODEL // LANES
MOD_ROWS = 24
VMEM_LIMIT_BYTES = 48 * 1024 * 1024

Geo = collections.namedtuple("Geo", "bp tp bs ts past")


def _n_prompt(geo):
    return geo.bp * geo.tp


def _n_rows(geo):
    return geo.bp * geo.tp + geo.bs * geo.ts


def _params(sem):
    return pltpu.CompilerParams(dimension_semantics=sem, vmem_limit_bytes=VMEM_LIMIT_BYTES)


def _log_sigmoid(x):
    return jnp.minimum(x, 0.0) - jnp.log1p(jnp.exp(-jnp.abs(x)))


def _store_token_major(ref, val):
    rows = val.shape[0]
    for j in range(LANE_GROUPS):
        ref[pl.ds(j, rows, stride=LANE_GROUPS), :] = val[:, j * LANES:(j + 1) * LANES]


def _load_token_major(ref, rows):
    return jnp.concatenate(
        [ref[pl.ds(j, rows, stride=LANE_GROUPS), :] for j in range(LANE_GROUPS)], axis=1)


def _split3(x):
    hi = x.astype(BF16)
    r1 = x - hi.astype(F32)
    mid = r1.astype(BF16)
    lo = (r1 - mid.astype(F32)).astype(BF16)
    return hi, mid, lo


def _mod_tables(mod, ncomp, geo):
    m = mod.reshape(mod.shape[0], ncomp, D_MODEL)
    tab_p = m[:geo.bp].transpose(1, 0, 2).reshape(ncomp, geo.bp, 1, D_MODEL)
    tab_s = jnp.repeat(m[geo.bp:geo.bp + geo.bs], geo.ts, axis=0).transpose(1, 0, 2)
    return tab_p, tab_s


def _mod_specs(comp, geo, tile_of, col_of, width):
    tiles_per_batch = geo.tp // ROW_TILE

    def idx_p(*ids):
        return (comp, jnp.minimum(tile_of(*ids) // tiles_per_batch, geo.bp - 1), 0, col_of(*ids))

    def idx_s(*ids):
        return (comp, 0, col_of(*ids))

    return (pl.BlockSpec((None, None, 1, width), idx_p),
            pl.BlockSpec((None, ROW_TILE, width), idx_s))


def _norm_mod_tile(x, g, sh_p, sc_p, sh_s, sc_s, is_prompt):
    ms = jnp.mean(x * x, axis=-1, keepdims=True)
    xn = x * lax.rsqrt(ms + EPS) * g
    shift = jnp.where(is_prompt, sh_p, sh_s)
    scale = jnp.where(is_prompt, sc_p, sc_s)
    return xn * (1.0 + scale) + shift


def _norm_mod_kernel(x_ref, g_ref, shp_ref, scp_ref, shs_ref, scs_ref, h_ref, *, n_prompt_tiles):
    is_prompt = pl.program_id(0) < n_prompt_tiles
    h_ref[...] = _norm_mod_tile(x_ref[...], g_ref[...], shp_ref[...], scp_ref[...],
                                shs_ref[...], scs_ref[...], is_prompt)


def _norm_mod(x, g, tabs, comp_shift, comp_scale, geo, name):
    n = x.shape[0]
    tab_p, tab_s = tabs
    row = lambda i: i
    col = lambda i: 0
    shp, shs = _mod_specs(comp_shift, geo, row, col, D_MODEL)
    scp, scs = _mod_specs(comp_scale, geo, row, col, D_MODEL)
    x_spec = pl.BlockSpec((ROW_TILE, D_MODEL), lambda i: (i, 0))
    return pl.pallas_call(
        functools.partial(_norm_mod_kernel, n_prompt_tiles=_n_prompt(geo) // ROW_TILE),
        out_shape=jax.ShapeDtypeStruct((n, D_MODEL), F32),
        grid=(n // ROW_TILE,),
        in_specs=[x_spec, pl.BlockSpec((1, D_MODEL), lambda i: (0, 0)), shp, scp, shs, scs],
        out_specs=x_spec,
        compiler_params=_params(("arbitrary",)),
        name=name,
    )(x, g.reshape(1, D_MODEL), tab_p, tab_p, tab_s, tab_s)


def _norm_router_kernel(x_ref, g_ref, shp_ref, scp_ref, shs_ref, scs_ref, wr_ref, br_ref,
                        h_ref, ti_ref, tw_ref, rk_ref, cnt_ref, run_ref, *, n_prompt_tiles):
    i = pl.program_id(0)

    @pl.when(i == 0)
    def _():
        run_ref[...] = jnp.zeros_like(run_ref)

    h = _norm_mod_tile(x_ref[...], g_ref[...], shp_ref[...], scp_ref[...],
                       shs_ref[...], scs_ref[...], i < n_prompt_tiles)
    _store_token_major(h_ref, h)
    logits = jnp.dot(h, wr_ref[...], preferred_element_type=F32) + br_ref[...]
    lane = lax.broadcasted_iota(I32, logits.shape, 1)
    vals, idxs = [], []
    cur = logits
    for _ in range(TOP_K):
        m = jnp.max(cur, axis=-1, keepdims=True)
        idx = jnp.min(jnp.where(cur == m, lane, N_EXPERTS), axis=-1, keepdims=True)
        vals.append(m)
        idxs.append(idx)
        cur = jnp.where(lane == idx, -jnp.inf, cur)
    exps = [jnp.exp(v - vals[0]) for v in vals]
    denom = exps[0] + exps[1] + exps[2] + exps[3]

    r_i = lax.broadcasted_iota(I32, (ROW_TILE, ROW_TILE), 0)
    c_i = lax.broadcasted_iota(I32, (ROW_TILE, ROW_TILE), 1)
    tril_excl = (c_i < r_i).astype(F32)
    base = run_ref[...]
    ranks = []
    for k in range(TOP_K):
        onehot = (lane == idxs[k]).astype(F32)
        prefix = jnp.dot(tril_excl, onehot, preferred_element_type=F32)
        ranks.append(jnp.sum(onehot * (prefix + base), axis=-1, keepdims=True))
        base = base + jnp.sum(onehot, axis=0, keepdims=True)
    run_ref[...] = base
    cnt_ref[...] = base

    lane_out = lax.broadcasted_iota(I32, (ROW_TILE, LANES), 1)
    ti = jnp.zeros((ROW_TILE, LANES), I32)
    tw = jnp.zeros((ROW_TILE, LANES), F32)
    rk = jnp.zeros((ROW_TILE, LANES), I32)
    for k in range(TOP_K):
        sel = lane_out == k
        ti = jnp.where(sel, idxs[k], ti)
        tw = jnp.where(sel, exps[k] / denom, tw)
        rk = jnp.where(sel, ranks[k].astype(I32), rk)
    ti_ref[...] = ti
    tw_ref[...] = tw
    rk_ref[...] = rk


def _norm_router(x, g, tabs, comp_shift, comp_scale, w_router, b_router, layer, geo):
    n = x.shape[0]
    tab_p, tab_s = tabs
    row = lambda i: i
    col = lambda i: 0
    shp, shs = _mod_specs(comp_shift, geo, row, col, D_MODEL)
    scp, scs = _mod_specs(comp_scale, geo, row, col, D_MODEL)
    x_spec = pl.BlockSpec((ROW_TILE, D_MODEL), lambda i: (i, 0))
    lane_spec = pl.BlockSpec((ROW_TILE, LANES), lambda i: (i, 0))
    tm_spec = pl.BlockSpec((ROW_TILE * LANE_GROUPS, LANES), lambda i: (i, 0))
    return pl.pallas_call(
        functools.partial(_norm_router_kernel, n_prompt_tiles=_n_prompt(geo) // ROW_TILE),
        out_shape=(jax.ShapeDtypeStruct((n * LANE_GROUPS, LANES), F32),
                   jax.ShapeDtypeStruct((n, LANES), I32),
                   jax.ShapeDtypeStruct((n, LANES), F32),
                   jax.ShapeDtypeStruct((n, LANES), I32),
                   jax.ShapeDtypeStruct((1, N_EXPERTS), F32)),
        grid=(n // ROW_TILE,),
        in_specs=[x_spec, pl.BlockSpec((1, D_MODEL), lambda i: (0, 0)), shp, scp, shs, scs,
                  pl.BlockSpec((None, D_MODEL, N_EXPERTS), lambda i: (layer, 0, 0)),
                  pl.BlockSpec((None, 1, N_EXPERTS), lambda i: (layer, 0, 0))],
        out_specs=(tm_spec, lane_spec, lane_spec, lane_spec,
                   pl.BlockSpec((1, N_EXPERTS), lambda i: (0, 0))),
        scratch_shapes=[pltpu.VMEM((1, N_EXPERTS), F32)],
        compiler_params=_params(("arbitrary",)),
        name=f"norm_router_{layer}",
    )(x, g.reshape(1, D_MODEL), tab_p, tab_p, tab_s, tab_s, w_router,
      b_router.reshape(DEPTH, 1, N_EXPERTS))


def _mm_kernel(*refs, epi, n_prompt_tiles):
    x_ref, w_ref = refs[0], refs[1]
    o_ref = refs[-1]
    acc = jnp.dot(x_ref[...], w_ref[...], preferred_element_type=F32)
    if epi == "bias":
        o_ref[...] = acc + refs[2][...]
    elif epi == "logsig":
        o_ref[...] = _log_sigmoid(acc + refs[2][...])
    elif epi == "headnorm":
        g = refs[2][...]
        for j in range(acc.shape[1] // HD_B):
            blk = acc[:, j * HD_B:(j + 1) * HD_B]
            ms = jnp.mean(blk * blk, axis=-1, keepdims=True)
            o_ref[:, j * HD_B:(j + 1) * HD_B] = blk * lax.rsqrt(ms + EPS) * g
    elif epi == "resgate":
        gate = jnp.where(pl.program_id(1) < n_prompt_tiles, refs[3][...], refs[4][...])
        o_ref[...] = refs[2][...] + gate * acc
    else:
        o_ref[...] = acc


def _mm(x, w, *, ncols, tn, name, wl=None, col0=0, tm=ROW_TILE, epi="none", bias=None, bl=0,
        gain=None, xres=None, gate_tabs=None, gate_comp=0, geo=None):
    m_rows, k_dim = x.shape
    cb0 = col0 // tn
    grid = (ncols // tn, m_rows // tm)
    x_spec = pl.BlockSpec((tm, k_dim), lambda n, m: (m, 0))
    if wl is None:
        w_spec = pl.BlockSpec((k_dim, tn), lambda n, m: (0, cb0 + n))
    else:
        w_spec = pl.BlockSpec((None, k_dim, tn), lambda n, m: (wl, 0, cb0 + n))
    o_spec = pl.BlockSpec((tm, tn), lambda n, m: (m, n))
    args, specs = [x, w], [x_spec, w_spec]
    n_prompt_tiles = 0
    if epi in ("bias", "logsig"):
        args.append(bias)
        specs.append(pl.BlockSpec((None, 1, tn), lambda n, m: (bl, 0, cb0 + n)))
    elif epi == "headnorm":
        args.append(gain.reshape(1, HD_B))
        specs.append(pl.BlockSpec((1, HD_B), lambda n, m: (0, 0)))
    elif epi == "resgate":
        n_prompt_tiles = _n_prompt(geo) // ROW_TILE
        gp, gs = _mod_specs(gate_comp, geo, lambda n, m: m, lambda n, m: n, tn)
        args += [xres, gate_tabs[0], gate_tabs[1]]
        specs += [o_spec, gp, gs]
    return pl.pallas_call(
        functools.partial(_mm_kernel, epi=epi, n_prompt_tiles=n_prompt_tiles),
        out_shape=jax.ShapeDtypeStruct((m_rows, ncols), F32),
        grid=grid,
        in_specs=specs,
        out_specs=o_spec,
        compiler_params=_params(("arbitrary", "arbitrary")),
        name=name,
    )(*args)


def _gla_kernel(*refs, blk, n_chunks, has_s0):
    if has_s0:
        (q_ref, k_ref, v_ref, gl_ref, wg2_ref, bg2_ref, r_ref, on_ref, s0_ref, _oprev,
         o_ref, sout_ref, s_scr) = refs
    else:
        (q_ref, k_ref, v_ref, gl_ref, wg2_ref, bg2_ref, r_ref, on_ref,
         o_ref, sout_ref, s_scr) = refs
    c = pl.program_id(2)

    @pl.when(c == 0)
    def _():
        if has_s0:
            s_scr[...] = s0_ref[...]
        else:
            s_scr[...] = jnp.zeros_like(s_scr)

    log_a = _log_sigmoid(jnp.dot(gl_ref[...], wg2_ref[...], preferred_element_type=F32)
                         + bg2_ref[...]) / GATE_NORMALIZER
    r_i = lax.broadcasted_iota(I32, (blk, blk), 0)
    c_i = lax.broadcasted_iota(I32, (blk, blk), 1)
    causal = c_i <= r_i
    tril = causal.astype(BF16)
    hi, mid, lo = _split3(log_a)
    tri_dot = lambda part: jnp.dot(tril, part, preferred_element_type=F32)
    g_cum = tri_dot(hi) + tri_dot(mid) + tri_dot(lo)
    g_last = g_cum[blk - 1:blk, :]

    q = q_ref[...] * (DK_A ** -0.5)
    k = k_ref[...]
    v = v_ref[...]
    s_prev = s_scr[...]
    q_dec = q * jnp.exp(g_cum)
    k_dec = k * jnp.exp(-g_cum)
    att = lax.dot_general(q_dec, k_dec, (((1,), (1,)), ((), ())), preferred_element_type=F32)
    att = jnp.where(causal, att, 0.0)
    o = (jnp.dot(att, v, preferred_element_type=F32)
         + jnp.dot(q_dec, s_prev, preferred_element_type=F32))
    k_end = k * jnp.exp(g_last - g_cum)
    decay_col = jnp.transpose(jnp.broadcast_to(jnp.exp(g_last), (LANES, DK_A)))[:, 0:1]
    s_new = s_prev * decay_col + lax.dot_general(
        k_end, v, (((0,), (0,)), ((), ())), preferred_element_type=F32)
    s_scr[...] = s_new

    @pl.when(c == n_chunks - 1)
    def _():
        sout_ref[...] = s_new

    ms = jnp.mean(o * o, axis=-1, keepdims=True)
    o_n = o * lax.rsqrt(ms + EPS) * on_ref[...]
    r = r_ref[...]
    o_ref[...] = o_n * (r * jax.nn.sigmoid(r))


def _gla_scan(qkv, gl, r, layer, w_g2, b_g2, onorm, s0, o_prev, *, row0, nb, t_len, n_rows, name):
    blk = min(CHUNK, t_len)
    n_chunks = t_len // blk
    rb0 = row0 // blk
    has_s0 = s0 is not None

    def rows(b, h, c):
        return rb0 + b * n_chunks + c

    v_col0 = 2 * DK_TOT // DV_A
    in_specs = [
        pl.BlockSpec((blk, DK_A), lambda b, h, c: (rows(b, h, c), h)),
        pl.BlockSpec((blk, DK_A), lambda b, h, c: (rows(b, h, c), H_A + h)),
        pl.BlockSpec((blk, DV_A), lambda b, h, c: (rows(b, h, c), v_col0 + h)),
        pl.BlockSpec((blk, GATE_RANK), lambda b, h, c: (rows(b, h, c), 0)),
        pl.BlockSpec((None, GATE_RANK, DK_A), lambda b, h, c: (layer, 0, h)),
        pl.BlockSpec((None, 1, DK_A), lambda b, h, c: (layer, 0, h)),
        pl.BlockSpec((blk, DV_A), lambda b, h, c: (rows(b, h, c), h)),
        pl.BlockSpec((None, 1, DV_A), lambda b, h, c: (layer, 0, 0)),
    ]
    args = [qkv, qkv, qkv, gl, w_g2, b_g2.reshape(N_A, 1, DK_TOT), r, onorm.reshape(N_A, 1, DV_A)]
    aliases = {}
    if has_s0:
        in_specs.append(pl.BlockSpec((None, None, None, DK_A, DV_A),
                                     lambda b, h, c: (layer, b, h, 0, 0)))
        in_specs.append(pl.BlockSpec(memory_space=pl.ANY))
        args += [s0, o_prev]
        aliases = {len(args) - 1: 0}
    return pl.pallas_call(
        functools.partial(_gla_kernel, blk=blk, n_chunks=n_chunks, has_s0=has_s0),
        out_shape=(jax.ShapeDtypeStruct((n_rows, DV_TOT), F32),
                   jax.ShapeDtypeStruct((nb, H_A, DK_A, DV_A), F32)),
        grid=(nb, H_A, n_chunks),
        in_specs=in_specs,
        out_specs=(pl.BlockSpec((blk, DV_A), lambda b, h, c: (rows(b, h, c), h)),
                   pl.BlockSpec((None, None, DK_A, DV_A), lambda b, h, c: (b, h, 0, 0))),
        scratch_shapes=[pltpu.VMEM((DK_A, DV_A), F32)],
        input_output_aliases=aliases,
        compiler_params=_params(("arbitrary", "arbitrary", "arbitrary")),
        name=name,
    )(*args)


def _cumsum_kernel(x_ref, o_ref, carry_ref, *, tb):
    @pl.when(pl.program_id(1) == 0)
    def _():
        carry_ref[...] = jnp.zeros_like(carry_ref)

    r_i = lax.broadcasted_iota(I32, (tb, tb), 0)
    c_i = lax.broadcasted_iota(I32, (tb, tb), 1)
    triu = (r_i <= c_i).astype(BF16)
    hi, mid, lo = _split3(x_ref[...])
    tri_dot = lambda part: jnp.dot(part, triu, preferred_element_type=F32)
    y = tri_dot(hi) + tri_dot(mid) + tri_dot(lo) + carry_ref[:, 0:1]
    o_ref[...] = y
    carry_ref[...] = jnp.broadcast_to(y[:, tb - 1:tb], carry_ref.shape)


def _cumsum_lanes(x, tb, name):
    nb, nh, t_len = x.shape
    spec = pl.BlockSpec((None, nh, tb), lambda b, j: (b, 0, j))
    return pl.pallas_call(
        functools.partial(_cumsum_kernel, tb=tb),
        out_shape=jax.ShapeDtypeStruct(x.shape, F32),
        grid=(nb, t_len // tb),
        in_specs=[spec],
        out_specs=spec,
        scratch_shapes=[pltpu.VMEM((nh, LANES), F32)],
        compiler_params=_params(("arbitrary", "arbitrary")),
        name=name,
    )(x)


ATT_TILE = 256


def _fox_prompt_kernel(q_ref, k_ref, v_ref, cq_ref, ck_ref, o_ref, m_ref, l_ref, acc_ref, *, n_k):
    qi = pl.program_id(1)
    ki = pl.program_id(2)

    @pl.when(ki == 0)
    def _():
        m_ref[...] = jnp.full_like(m_ref, -jnp.inf)
        l_ref[...] = jnp.zeros_like(l_ref)
        acc_ref[...] = jnp.zeros_like(acc_ref)

    @pl.when(ki <= qi)
    def _():
        qpos = qi * ATT_TILE + lax.broadcasted_iota(I32, (ATT_TILE, ATT_TILE), 0)
        kpos = ki * ATT_TILE + lax.broadcasted_iota(I32, (ATT_TILE, ATT_TILE), 1)
        visible = kpos <= qpos
        for h in range(H_B):
            cols = slice(h * HD_B, (h + 1) * HD_B)
            s = lax.dot_general(q_ref[:, cols], k_ref[:, cols], (((1,), (1,)), ((), ())),
                                preferred_element_type=F32) * (HD_B ** -0.5)
            s = s + cq_ref[:, h:h + 1] - ck_ref[h:h + 1, :]
            s = jnp.where(visible, s, -jnp.inf)
            m_prev = m_ref[h]
            m_next = jnp.maximum(m_prev, jnp.max(s, axis=1, keepdims=True))
            alpha = jnp.exp(m_prev - m_next)
            p = jnp.exp(s - jnp.concatenate([m_next] * (ATT_TILE // LANES), axis=1))
            l_ref[h] = alpha * l_ref[h] + jnp.sum(p, axis=1, keepdims=True)
            acc_ref[:, cols] = acc_ref[:, cols] * alpha + jnp.dot(
                p, v_ref[:, cols], preferred_element_type=F32)
            m_ref[h] = m_next

    @pl.when(ki == n_k - 1)
    def _():
        for h in range(H_B):
            cols = slice(h * HD_B, (h + 1) * HD_B)
            o_ref[:, cols] = acc_ref[:, cols] / l_ref[h]


def _fox_prompt(q, k, v, cq, ck_t, geo, n_rows, name):
    n_q = geo.tp // ATT_TILE
    kv_blk = lambda b, qi, ki: (b * n_q + jnp.minimum(ki, qi), 0)
    q_blk = lambda b, qi, ki: (b * n_q + qi, 0)
    return pl.pallas_call(
        functools.partial(_fox_prompt_kernel, n_k=n_q),
        out_shape=jax.ShapeDtypeStruct((n_rows, D_MODEL), F32),
        grid=(geo.bp, n_q, n_q),
        in_specs=[pl.BlockSpec((ATT_TILE, D_MODEL), q_blk),
                  pl.BlockSpec((ATT_TILE, D_MODEL), kv_blk),
                  pl.BlockSpec((ATT_TILE, D_MODEL), kv_blk),
                  pl.BlockSpec((ATT_TILE, H_B), q_blk),
                  pl.BlockSpec((None, H_B, ATT_TILE),
                               lambda b, qi, ki: (b, 0, jnp.minimum(ki, qi)))],
        out_specs=pl.BlockSpec((ATT_TILE, D_MODEL), q_blk),
        scratch_shapes=[pltpu.VMEM((H_B, ATT_TILE, LANES), F32),
                        pltpu.VMEM((H_B, ATT_TILE, LANES), F32),
                        pltpu.VMEM((ATT_TILE, D_MODEL), F32)],
        compiler_params=_params(("arbitrary", "arbitrary", "arbitrary")),
        name=name,
    )(q, k, v, cq, ck_t)


def _fox_sample_kernel(q_ref, kp_ref, vp_ref, kn_ref, vn_ref, cq_ref, ckp_ref, ckn_ref, _oprev,
                       o_ref):
    t_len = q_ref.shape[0]
    scale = HD_B ** -0.5
    nt_dims = (((1,), (1,)), ((), ()))
    r_i = lax.broadcasted_iota(I32, (t_len, t_len), 0)
    c_i = lax.broadcasted_iota(I32, (t_len, t_len), 1)
    causal = c_i <= r_i
    for h in range(H_B):
        cols = slice(h * HD_B, (h + 1) * HD_B)
        q = q_ref[:, cols]
        cq = cq_ref[:, h:h + 1]
        s_past = lax.dot_general(q, kp_ref[:, cols], nt_dims, preferred_element_type=F32) * scale
        s_past = s_past + cq - ckp_ref[h:h + 1, :]
        s_new = lax.dot_general(q, kn_ref[:, cols], nt_dims, preferred_element_type=F32) * scale
        s_new = jnp.where(causal, s_new + cq - ckn_ref[h:h + 1, :], -jnp.inf)
        m = jnp.maximum(jnp.max(s_past, axis=1, keepdims=True),
                        jnp.max(s_new, axis=1, keepdims=True))
        p_past = jnp.exp(s_past - m)
        p_new = jnp.exp(s_new - m)
        denom = jnp.sum(p_past, axis=1, keepdims=True) + jnp.sum(p_new, axis=1, keepdims=True)
        o = (jnp.dot(p_past, vp_ref[:, cols], preferred_element_type=F32)
             + jnp.dot(p_new, vn_ref[:, cols], preferred_element_type=F32))
        o_ref[:, cols] = o / denom


def _fox_sample(q, cache_k, cache_v, k_new, v_new, cq, ck_past, ck_new, o_prev, geo, name):
    rb0 = _n_prompt(geo) // geo.ts
    new_blk = pl.BlockSpec((geo.ts, D_MODEL), lambda b: (rb0 + b, 0))
    past_blk = pl.BlockSpec((None, geo.past, D_MODEL), lambda b: (b, 0, 0))
    return pl.pallas_call(
        _fox_sample_kernel,
        out_shape=jax.ShapeDtypeStruct(o_prev.shape, F32),
        grid=(geo.bs,),
        in_specs=[new_blk, past_blk, past_blk, new_blk, new_blk,
                  pl.BlockSpec((None, geo.ts, H_B), lambda b: (b, 0, 0)),
                  pl.BlockSpec((None, H_B, geo.past), lambda b: (b, 0, 0)),
                  pl.BlockSpec((None, H_B, geo.ts), lambda b: (b, 0, 0)),
                  pl.BlockSpec(memory_space=pl.ANY)],
        out_specs=new_blk,
        input_output_aliases={8: 0},
        compiler_params=_params(("arbitrary",)),
        name=name,
    )(q, cache_k, cache_v, k_new, v_new, cq, ck_past, ck_new, o_prev)


def _max_tiles(n_rows):
    return -(-(n_rows * TOP_K + N_EXPERTS * (ROW_TILE - 1)) // ROW_TILE)


def _route_plan(top_i, rank, counts, n_rows):
    counts = counts.reshape(N_EXPERTS).astype(I32)
    padded = ((counts + ROW_TILE - 1) // ROW_TILE) * ROW_TILE
    g_end = jnp.cumsum(padded)
    g_start = g_end - padded
    onehot = top_i[..., None] == jnp.arange(N_EXPERTS, dtype=I32)
    pos = jnp.sum(jnp.where(onehot, g_start, 0), axis=-1) + rank
    n_tiles = g_end[-1] // ROW_TILE
    max_tiles = _max_tiles(n_rows)
    tile_ids = jnp.arange(max_tiles, dtype=I32)
    tile_e = jnp.sum(((tile_ids * ROW_TILE)[:, None] >= g_end[None, :]).astype(I32), axis=1)
    last_e = jnp.sum((((n_tiles - 1) * ROW_TILE) >= g_end).astype(I32))
    tile_e = jnp.where(tile_ids < n_tiles, tile_e, last_e).astype(I32)
    return (pos.reshape(-1).astype(I32), tile_e, n_tiles.reshape(1).astype(I32),
            (g_start + counts).astype(I32), (padded - counts).astype(I32))


def _token_slab(ref, token):
    return ref.at[pl.ds(pl.multiple_of(token * LANE_GROUPS, LANE_GROUPS), LANE_GROUPS), :]


def _token_copy(src, src_token, dst, dst_token, sem):
    return pltpu.make_async_copy(_token_slab(src, src_token), _token_slab(dst, dst_token), sem)


def _dispatch_kernel(pos_ref, pad0_ref, padn_ref, h_ref, xs_ref, zero_ref, sem):
    i = pl.program_id(0)

    @pl.when(i == 0)
    def _():
        zero_ref[...] = jnp.zeros_like(zero_ref)

        def per_expert(e, carry):
            start = pad0_ref[e]
            count = padn_ref[e]

            def issue(j, c):
                _token_copy(zero_ref, 0, xs_ref, start + j, sem).start()
                return c

            def drain(j, c):
                _token_copy(zero_ref, 0, xs_ref, start + j, sem).wait()
                return c

            lax.fori_loop(0, count, issue, 0)
            lax.fori_loop(0, count, drain, 0)
            return carry

        lax.fori_loop(0, N_EXPERTS, per_expert, 0)

    def issue_row(r, c):
        token = i * ROW_TILE + r
        for k in range(TOP_K):
            _token_copy(h_ref, token, xs_ref, pos_ref[token * TOP_K + k], sem).start()
        return c

    def drain_row(r, c):
        token = i * ROW_TILE + r
        for k in range(TOP_K):
            _token_copy(h_ref, token, xs_ref, pos_ref[token * TOP_K + k], sem).wait()
        return c

    lax.fori_loop(0, ROW_TILE, issue_row, 0)
    lax.fori_loop(0, ROW_TILE, drain_row, 0)


def _dispatch(h_tm, pos, pad0, padn, layer):
    n = h_tm.shape[0] // LANE_GROUPS
    rows_sorted = _max_tiles(n) * ROW_TILE
    grid_spec = pltpu.PrefetchScalarGridSpec(
        num_scalar_prefetch=3,
        grid=(n // ROW_TILE,),
        in_specs=[pl.BlockSpec(memory_space=pl.ANY)],
        out_specs=pl.BlockSpec(memory_space=pl.ANY),
        scratch_shapes=[pltpu.VMEM((LANE_GROUPS, LANES), F32), pltpu.SemaphoreType.DMA],
    )
    return pl.pallas_call(
        _dispatch_kernel,
        out_shape=jax.ShapeDtypeStruct((rows_sorted * LANE_GROUPS, LANES), F32),
        grid_spec=grid_spec,
        compiler_params=_params(("arbitrary",)),
        name=f"moe_dispatch_{layer}",
    )(pos, pad0, padn, h_tm)


FF_TILE = 512


def _moe_up_kernel(te_ref, nt_ref, xs_ref, wg_ref, wu_ref, bg_ref, bu_ref, a_ref):
    @pl.when(pl.program_id(1) < nt_ref[0])
    def _():
        x = _load_token_major(xs_ref, ROW_TILE)
        g = jnp.dot(x, wg_ref[...], preferred_element_type=F32) + bg_ref[...]
        u = jnp.dot(x, wu_ref[...], preferred_element_type=F32) + bu_ref[...]
        g = jnp.minimum(g, SWIGLU_LIMIT)
        u = jnp.clip(u, -SWIGLU_LIMIT, SWIGLU_LIMIT)
        a_ref[...] = g * jax.nn.sigmoid(SWIGLU_ALPHA * g) * (u + 1.0)


def _moe_down_kernel(te_ref, nt_ref, a_ref, w_ref, b_ref, y_ref):
    @pl.when(pl.program_id(1) < nt_ref[0])
    def _():
        y = jnp.dot(a_ref[...], w_ref[...], preferred_element_type=F32) + b_ref[...]
        _store_token_major(y_ref, y)


def _moe_experts(xs_tm, tile_e, n_tiles, w_in, b_in, w_out, b_out, layer):
    rows_sorted = xs_tm.shape[0] // LANE_GROUPS
    max_tiles = rows_sorted // ROW_TILE
    u_blk0 = D_FF // FF_TILE
    row_blk = lambda n, r, te, nt: (jnp.minimum(r, nt[0] - 1), 0)
    tm_spec = pl.BlockSpec((ROW_TILE * LANE_GROUPS, LANES), row_blk)
    up_spec = pltpu.PrefetchScalarGridSpec(
        num_scalar_prefetch=2,
        grid=(D_FF // FF_TILE, max_tiles),
        in_specs=[
            tm_spec,
            pl.BlockSpec((None, None, D_MODEL, FF_TILE), lambda n, r, te, nt: (layer, te[r], 0, n)),
            pl.BlockSpec((None, None, D_MODEL, FF_TILE),
                         lambda n, r, te, nt: (layer, te[r], 0, u_blk0 + n)),
            pl.BlockSpec((None, None, 1, FF_TILE), lambda n, r, te, nt: (layer, te[r], 0, n)),
            pl.BlockSpec((None, None, 1, FF_TILE),
                         lambda n, r, te, nt: (layer, te[r], 0, u_blk0 + n)),
        ],
        out_specs=pl.BlockSpec((ROW_TILE, FF_TILE),
                               lambda n, r, te, nt: (jnp.minimum(r, nt[0] - 1), n)),
    )
    b_in4 = b_in.reshape(DEPTH, N_EXPERTS, 1, 2 * D_FF)
    act = pl.pallas_call(
        _moe_up_kernel,
        out_shape=jax.ShapeDtypeStruct((rows_sorted, D_FF), F32),
        grid_spec=up_spec,
        compiler_params=_params(("arbitrary", "arbitrary")),
        name=f"moe_up_{layer}",
    )(tile_e, n_tiles, xs_tm, w_in, w_in, b_in4, b_in4)
    down_spec = pltpu.PrefetchScalarGridSpec(
        num_scalar_prefetch=2,
        grid=(1, max_tiles),
        in_specs=[
            pl.BlockSpec((ROW_TILE, D_FF), row_blk),
            pl.BlockSpec((None, None, D_FF, D_MODEL), lambda n, r, te, nt: (layer, te[r], 0, 0)),
            pl.BlockSpec((None, None, 1, D_MODEL), lambda n, r, te, nt: (layer, te[r], 0, 0)),
        ],
        out_specs=tm_spec,
    )
    return pl.pallas_call(
        _moe_down_kernel,
        out_shape=jax.ShapeDtypeStruct((rows_sorted * LANE_GROUPS, LANES), F32),
        grid_spec=down_spec,
        compiler_params=_params(("arbitrary", "arbitrary")),
        name=f"moe_down_{layer}",
    )(tile_e, n_tiles, act, w_out, b_out.reshape(DEPTH, N_EXPERTS, 1, D_MODEL))


def _combine_kernel(pos_ref, ys_ref, tw_ref, x_ref, gp_ref, gs_ref, o_ref, buf_ref, sem, *,
                    n_prompt_tiles):
    i = pl.program_id(0)

    def issue_row(r, c):
        for k in range(TOP_K):
            p = pos_ref[(i * ROW_TILE + r) * TOP_K + k]
            _token_copy(ys_ref, p, buf_ref.at[k], r, sem).start()
        return c

    def drain_row(r, c):
        for k in range(TOP_K):
            p = pos_ref[(i * ROW_TILE + r) * TOP_K + k]
            _token_copy(ys_ref, p, buf_ref.at[k], r, sem).wait()
        return c

    lax.fori_loop(0, ROW_TILE, issue_row, 0)
    lax.fori_loop(0, ROW_TILE, drain_row, 0)
    tw = tw_ref[...]
    y = tw[:, 0:1] * _load_token_major(buf_ref.at[0], ROW_TILE)
    for k in range(1, TOP_K):
        y = y + tw[:, k:k + 1] * _load_token_major(buf_ref.at[k], ROW_TILE)
    gate = jnp.where(i < n_prompt_tiles, gp_ref[...], gs_ref[...])
    o_ref[...] = x_ref[...] + gate * y


def _combine(ys, pos, top_w, x, gate_tabs, gate_comp, geo, layer):
    n = x.shape[0]
    gp, gs = _mod_specs(gate_comp, geo, lambda i, *_: i, lambda i, *_: 0, D_MODEL)
    x_spec = pl.BlockSpec((ROW_TILE, D_MODEL), lambda i, *_: (i, 0))
    grid_spec = pltpu.PrefetchScalarGridSpec(
        num_scalar_prefetch=1,
        grid=(n // ROW_TILE,),
        in_specs=[pl.BlockSpec(memory_space=pl.ANY),
                  pl.BlockSpec((ROW_TILE, LANES), lambda i, *_: (i, 0)),
                  x_spec, gp, gs],
        out_specs=x_spec,
        scratch_shapes=[pltpu.VMEM((TOP_K, ROW_TILE * LANE_GROUPS, LANES), F32),
                        pltpu.SemaphoreType.DMA],
    )
    return pl.pallas_call(
        functools.partial(_combine_kernel, n_prompt_tiles=_n_prompt(geo) // ROW_TILE),
        out_shape=jax.ShapeDtypeStruct((n, D_MODEL), F32),
        grid_spec=grid_spec,
        compiler_params=_params(("arbitrary",)),
        name=f"moe_combine_{layer}",
    )(pos, ys, top_w, x, gate_tabs[0], gate_tabs[1])


def _trunk(x, cs, state_gla, cache_k, cache_v, cache_logf, geo,
           w_mod, b_mod, norm_mix, norm_ffn,
           w_gla_in, w_gla_g2, b_gla_g2, gla_onorm, w_gla_out,
           w_kv_mod, b_kv_mod, norm_kv, w_kv, b_f, k_norm,
           w_fox_q, q_norm, w_fox_out,
           w_router, b_router, w_moe_in, b_moe_in, w_moe_out, b_moe_out):
    n = _n_rows(geo)
    n_p = _n_prompt(geo)
    b_mod3 = b_mod.reshape(DEPTH, 1, 6 * D_MODEL)
    gla_p, gla_s = [], []
    kv = None
    for layer in range(DEPTH):
        if layer == N_A:
            kv_mod = _mm(cs, w_kv_mod, ncols=2 * D_MODEL, tn=1024, tm=MOD_ROWS, epi="bias",
                         bias=b_kv_mod.reshape(1, 1, 2 * D_MODEL), name="kv_mod")
            kv_tabs = _mod_tables(kv_mod, 2, geo)
            hk = _norm_mod(x, norm_kv, kv_tabs, 0, 1, geo, "norm_kv")
            k_new = _mm(hk, w_kv, ncols=D_MODEL, tn=1024, epi="headnorm", gain=k_norm, name="kv_k")
            v_new = _mm(hk, w_kv, ncols=D_MODEL, tn=1024, col0=D_MODEL, name="kv_v")
            logf = _mm(hk, w_kv[:, 2 * D_MODEL:], ncols=H_B, tn=H_B, epi="logsig",
                       bias=b_f.reshape(1, 1, H_B), name="kv_f")
            logf_p = logf[:n_p].reshape(geo.bp, geo.tp, H_B)
            logf_s = logf[n_p:].reshape(geo.bs, geo.ts, H_B)
            c_p_t = _cumsum_lanes(logf_p.transpose(0, 2, 1), 512, "cumsum_p")
            cq_p = c_p_t.transpose(0, 2, 1).reshape(n_p, H_B)
            all_s = jnp.concatenate([cache_logf.astype(F32), logf_s], axis=1).transpose(0, 2, 1)
            c_s_t = _cumsum_lanes(all_s, geo.past + geo.ts, "cumsum_s")
            ck_past = c_s_t[:, :, :geo.past]
            ck_new = c_s_t[:, :, geo.past:]
            cq_s = ck_new.transpose(0, 2, 1)
            cache_k2 = cache_k.reshape(geo.bs, geo.past, D_MODEL)
            cache_v2 = cache_v.reshape(geo.bs, geo.past, D_MODEL)
            kv = (k_new, v_new, logf_p, logf_s)

        mod = _mm(cs, w_mod, wl=layer, ncols=6 * D_MODEL, tn=1024, tm=MOD_ROWS, epi="bias",
                  bias=b_mod3, bl=layer, name=f"mod_{layer}")
        tabs = _mod_tables(mod, 6, geo)
        h = _norm_mod(x, norm_mix[layer], tabs, 0, 1, geo, f"norm_mix_{layer}")
        if layer < N_A:
            qkv = _mm(h, w_gla_in, wl=layer, ncols=2 * DK_TOT + DV_TOT, tn=1024, name=f"gla_qkv_{layer}")
            gcol = 2 * DK_TOT + DV_TOT
            gl = _mm(h, w_gla_in[layer, :, gcol:gcol + GATE_RANK], ncols=GATE_RANK, tn=GATE_RANK,
                     name=f"gla_gl_{layer}")
            r = _mm(h, w_gla_in[layer, :, gcol + GATE_RANK:], ncols=DV_TOT, tn=1024,
                    name=f"gla_r_{layer}")
            o, s_p = _gla_scan(qkv, gl, r, layer, w_gla_g2, b_gla_g2, gla_onorm, None, None,
                               row0=0, nb=geo.bp, t_len=geo.tp, n_rows=n, name=f"gla_scan_p_{layer}")
            o, s_s = _gla_scan(qkv, gl, r, layer, w_gla_g2, b_gla_g2, gla_onorm, state_gla, o,
                               row0=n_p, nb=geo.bs, t_len=geo.ts, n_rows=n, name=f"gla_scan_s_{layer}")
            gla_p.append(s_p)
            gla_s.append(s_s)
            x = _mm(o, w_gla_out, wl=layer, ncols=D_MODEL, tn=1024, epi="resgate", xres=x,
                    gate_tabs=tabs, gate_comp=2, geo=geo, name=f"gla_out_{layer}")
        else:
            j = layer - N_A
            q = _mm(h, w_fox_q, wl=j, ncols=D_MODEL, tn=1024, epi="headnorm", gain=q_norm[j],
                    name=f"fox_q_{j}")
            o = _fox_prompt(q, kv[0], kv[1], cq_p, c_p_t, geo, n, f"fox_attn_p_{j}")
            o = _fox_sample(q, cache_k2, cache_v2, kv[0], kv[1], cq_s, ck_past, ck_new, o, geo,
                            f"fox_attn_s_{j}")
            x = _mm(o, w_fox_out, wl=j, ncols=D_MODEL, tn=1024, epi="resgate", xres=x,
                    gate_tabs=tabs, gate_comp=2, geo=geo, name=f"fox_out_{j}")

        h2, top_i, top_w, rank, counts = _norm_router(x, norm_ffn[layer], tabs, 3, 4,
                                                      w_router, b_router, layer, geo)
        pos, tile_e, n_tiles, pad0, padn = _route_plan(top_i[:, :TOP_K], rank[:, :TOP_K], counts, n)
        xs = _dispatch(h2, pos, pad0, padn, layer)
        ys = _moe_experts(xs, tile_e, n_tiles, w_moe_in, b_moe_in, w_moe_out, b_moe_out, layer)
        x = _combine(ys, pos, top_w, x, tabs, 5, geo, layer)
    return x, gla_p, gla_s, kv


def kernel(x_prompt, x_sample, c_prompt, c_sample, state_gla, cache_k, cache_v, cache_logf, w_mod, b_mod, norm_mix, norm_ffn, w_gla_in, w_gla_g2, b_gla_g2, gla_onorm, w_gla_out, w_kv_mod, b_kv_mod, norm_kv, w_kv, b_f, k_norm, w_fox_q, q_norm, w_fox_out, w_router, b_router, w_moe_in, b_moe_in, w_moe_out, b_moe_out):
    bp, tp, _ = x_prompt.shape
    bs, ts, _ = x_sample.shape
    geo = Geo(bp, tp, bs, ts, cache_k.shape[1])
    assert bs * ts == ROW_TILE and tp % ROW_TILE == 0 and tp % 512 == 0 and bp + bs <= MOD_ROWS
    n_p = bp * tp
    x = jnp.concatenate([x_prompt.reshape(n_p, D_MODEL), x_sample.reshape(bs * ts, D_MODEL)], axis=0)
    c = jnp.concatenate([c_prompt, c_sample], axis=0).astype(F32)
    cs = jnp.pad(c * jax.nn.sigmoid(c), ((0, MOD_ROWS - bp - bs), (0, 0)))
    x, gla_p, gla_s, (k_new, v_new, logf_p, logf_s) = _trunk(
        x, cs, state_gla, cache_k, cache_v, cache_logf, geo,
        w_mod, b_mod, norm_mix, norm_ffn,
        w_gla_in, w_gla_g2, b_gla_g2, gla_onorm, w_gla_out,
        w_kv_mod, b_kv_mod, norm_kv, w_kv, b_f, k_norm,
        w_fox_q, q_norm, w_fox_out,
        w_router, b_router, w_moe_in, b_moe_in, w_moe_out, b_moe_out)
    return (x[:n_p].reshape(bp, tp, D_MODEL),
            x[n_p:].reshape(bs, ts, D_MODEL),
            jnp.stack(gla_p), jnp.stack(gla_s),
            k_new[:n_p].reshape(bp, tp, H_B, HD_B),
            v_new[:n_p].reshape(bp, tp, H_B, HD_B),
            logf_p,
            k_new[n_p:].reshape(bs, ts, H_B, HD_B),
            v_new[n_p:].reshape(bs, ts, H_B, HD_B),
            logf_s)
```

```python
import collections
import functools

import jax
import jax.numpy as jnp
from jax import lax
from jax.experimental import pallas as pl
from jax.experimental.pallas import tpu as pltpu

F32 = jnp.float32
BF16 = jnp.bfloat16
I32 = jnp.int32

D_MODEL = 2048
DEPTH = 4
N_A = 2
H_A = 4
DK_A = 256
DV_A = 512
DK_TOT = H_A * DK_A
DV_TOT = H_A * DV_A
GATE_RANK = 16
GATE_NORMALIZER = 16.0
H_B = 16
HD_B = 128
N_EXPERTS = 32
TOP_K = 4
D_FF = D_MODEL
SWIGLU_LIMIT = 7.0
SWIGLU_ALPHA = 1.702
EPS = 1e-6
CHUNK = 64

ROW_TILE = 256
LANES = 128
SUBLANES = 8
SLAB_ROWS = SUBLANES
SLAB_PARTS = D_MODEL // (SLAB_ROWS * LANES)


def _tm_shape(rows):
    return (SLAB_PARTS, rows * SLAB_ROWS, LANES)
MOD_ROWS = 24
VMEM_LIMIT_BYTES = 48 * 1024 * 1024

Geo = collections.namedtuple("Geo", "bp tp bs ts past")


def _n_prompt(geo):
    return geo.bp * geo.tp


def _n_rows(geo):
    return geo.bp * geo.tp + geo.bs * geo.ts


def _params(sem):
    return pltpu.CompilerParams(dimension_semantics=sem, vmem_limit_bytes=VMEM_LIMIT_BYTES)


def _log_sigmoid(x):
    return jnp.minimum(x, 0.0) - jnp.log1p(jnp.exp(-jnp.abs(x)))


def _store_token_major(ref, val):
    rows = val.shape[0]
    for j in range(SLAB_ROWS):
        for p in range(SLAB_PARTS):
            g = j * SLAB_PARTS + p
            ref[p, pl.ds(j, rows, stride=SLAB_ROWS), :] = val[:, g * LANES:(g + 1) * LANES]


def _load_token_major(ref, rows):
    return jnp.concatenate(
        [ref[p, pl.ds(j, rows, stride=SLAB_ROWS), :]
         for j in range(SLAB_ROWS) for p in range(SLAB_PARTS)], axis=1)


def _split3(x):
    hi = x.astype(BF16)
    r1 = x - hi.astype(F32)
    mid = r1.astype(BF16)
    lo = (r1 - mid.astype(F32)).astype(BF16)
    return hi, mid, lo


def _mod_tables(mod, ncomp, geo):
    m = mod.reshape(mod.shape[0], ncomp, D_MODEL)
    tab_p = m[:geo.bp].transpose(1, 0, 2).reshape(ncomp, geo.bp, 1, D_MODEL)
    tab_s = jnp.repeat(m[geo.bp:geo.bp + geo.bs], geo.ts, axis=0).transpose(1, 0, 2)
    return tab_p, tab_s


def _mod_specs(comp, geo, tile_of, col_of, width):
    tiles_per_batch = geo.tp // ROW_TILE

    def idx_p(*ids):
        return (comp, jnp.minimum(tile_of(*ids) // tiles_per_batch, geo.bp - 1), 0, col_of(*ids))

    def idx_s(*ids):
        return (comp, 0, col_of(*ids))

    return (pl.BlockSpec((None, None, 1, width), idx_p),
            pl.BlockSpec((None, ROW_TILE, width), idx_s))


def _norm_mod_tile(x, g, sh_p, sc_p, sh_s, sc_s, is_prompt):
    ms = jnp.mean(x * x, axis=-1, keepdims=True)
    xn = x * lax.rsqrt(ms + EPS) * g
    shift = jnp.where(is_prompt, sh_p, sh_s)
    scale = jnp.where(is_prompt, sc_p, sc_s)
    return xn * (1.0 + scale) + shift


def _norm_mod_kernel(x_ref, g_ref, shp_ref, scp_ref, shs_ref, scs_ref, h_ref, *, n_prompt_tiles):
    is_prompt = pl.program_id(0) < n_prompt_tiles
    h_ref[...] = _norm_mod_tile(x_ref[...], g_ref[...], shp_ref[...], scp_ref[...],
                                shs_ref[...], scs_ref[...], is_prompt)


def _norm_mod(x, g, tabs, comp_shift, comp_scale, geo, name):
    n = x.shape[0]
    tab_p, tab_s = tabs
    row = lambda i: i
    col = lambda i: 0
    shp, shs = _mod_specs(comp_shift, geo, row, col, D_MODEL)
    scp, scs = _mod_specs(comp_scale, geo, row, col, D_MODEL)
    x_spec = pl.BlockSpec((ROW_TILE, D_MODEL), lambda i: (i, 0))
    return pl.pallas_call(
        functools.partial(_norm_mod_kernel, n_prompt_tiles=_n_prompt(geo) // ROW_TILE),
        out_shape=jax.ShapeDtypeStruct((n, D_MODEL), F32),
        grid=(n // ROW_TILE,),
        in_specs=[x_spec, pl.BlockSpec((1, D_MODEL), lambda i: (0, 0)), shp, scp, shs, scs],
        out_specs=x_spec,
        compiler_params=_params(("arbitrary",)),
        name=name,
    )(x, g.reshape(1, D_MODEL), tab_p, tab_p, tab_s, tab_s)


def _norm_router_kernel(x_ref, g_ref, shp_ref, scp_ref, shs_ref, scs_ref, wr_ref, br_ref,
                        h_ref, ti_ref, tw_ref, rk_ref, cnt_ref, run_ref, *, n_prompt_tiles):
    i = pl.program_id(0)

    @pl.when(i == 0)
    def _():
        run_ref[...] = jnp.zeros_like(run_ref)

    h = _norm_mod_tile(x_ref[...], g_ref[...], shp_ref[...], scp_ref[...],
                       shs_ref[...], scs_ref[...], i < n_prompt_tiles)
    _store_token_major(h_ref, h)
    logits = jnp.dot(h, wr_ref[...], preferred_element_type=F32) + br_ref[...]
    lane = lax.broadcasted_iota(I32, logits.shape, 1)
    vals, idxs = [], []
    cur = logits
    for _ in range(TOP_K):
        m = jnp.max(cur, axis=-1, keepdims=True)
        idx = jnp.min(jnp.where(cur == m, lane, N_EXPERTS), axis=-1, keepdims=True)
        vals.append(m)
        idxs.append(idx)
        cur = jnp.where(lane == idx, -jnp.inf, cur)
    exps = [jnp.exp(v - vals[0]) for v in vals]
    denom = exps[0] + exps[1] + exps[2] + exps[3]

    r_i = lax.broadcasted_iota(I32, (ROW_TILE, ROW_TILE), 0)
    c_i = lax.broadcasted_iota(I32, (ROW_TILE, ROW_TILE), 1)
    tril_excl = (c_i < r_i).astype(F32)
    base = run_ref[...]
    ranks = []
    for k in range(TOP_K):
        onehot = (lane == idxs[k]).astype(F32)
        prefix = jnp.dot(tril_excl, onehot, preferred_element_type=F32)
        ranks.append(jnp.sum(onehot * (prefix + base), axis=-1, keepdims=True))
        base = base + jnp.sum(onehot, axis=0, keepdims=True)
    run_ref[...] = base
    cnt_ref[...] = base

    lane_out = lax.broadcasted_iota(I32, (ROW_TILE, LANES), 1)
    ti = jnp.zeros((ROW_TILE, LANES), I32)
    tw = jnp.zeros((ROW_TILE, LANES), F32)
    rk = jnp.zeros((ROW_TILE, LANES), I32)
    for k in range(TOP_K):
        sel = lane_out == k
        ti = jnp.where(sel, idxs[k], ti)
        tw = jnp.where(sel, exps[k] / denom, tw)
        rk = jnp.where(sel, ranks[k].astype(I32), rk)
    ti_ref[...] = ti
    tw_ref[...] = tw
    rk_ref[...] = rk


def _norm_router(x, g, tabs, comp_shift, comp_scale, w_router, b_router, layer, geo):
    n = x.shape[0]
    tab_p, tab_s = tabs
    row = lambda i: i
    col = lambda i: 0
    shp, shs = _mod_specs(comp_shift, geo, row, col, D_MODEL)
    scp, scs = _mod_specs(comp_scale, geo, row, col, D_MODEL)
    x_spec = pl.BlockSpec((ROW_TILE, D_MODEL), lambda i: (i, 0))
    lane_spec = pl.BlockSpec((ROW_TILE, LANES), lambda i: (i, 0))
    tm_spec = pl.BlockSpec(_tm_shape(ROW_TILE), lambda i: (0, i, 0))
    return pl.pallas_call(
        functools.partial(_norm_router_kernel, n_prompt_tiles=_n_prompt(geo) // ROW_TILE),
        out_shape=(jax.ShapeDtypeStruct(_tm_shape(n), F32),
                   jax.ShapeDtypeStruct((n, LANES), I32),
                   jax.ShapeDtypeStruct((n, LANES), F32),
                   jax.ShapeDtypeStruct((n, LANES), I32),
                   jax.ShapeDtypeStruct((1, N_EXPERTS), F32)),
        grid=(n // ROW_TILE,),
        in_specs=[x_spec, pl.BlockSpec((1, D_MODEL), lambda i: (0, 0)), shp, scp, shs, scs,
                  pl.BlockSpec((None, D_MODEL, N_EXPERTS), lambda i: (layer, 0, 0)),
                  pl.BlockSpec((None, 1, N_EXPERTS), lambda i: (layer, 0, 0))],
        out_specs=(tm_spec, lane_spec, lane_spec, lane_spec,
                   pl.BlockSpec((1, N_EXPERTS), lambda i: (0, 0))),
        scratch_shapes=[pltpu.VMEM((1, N_EXPERTS), F32)],
        compiler_params=_params(("arbitrary",)),
        name=f"norm_router_{layer}",
    )(x, g.reshape(1, D_MODEL), tab_p, tab_p, tab_s, tab_s, w_router,
      b_router.reshape(DEPTH, 1, N_EXPERTS))


def _mm_kernel(*refs, epi, n_prompt_tiles):
    x_ref, w_ref = refs[0], refs[1]
    o_ref = refs[-1]
    acc = jnp.dot(x_ref[...], w_ref[...], preferred_element_type=F32)
    if epi == "bias":
        o_ref[...] = acc + refs[2][...]
    elif epi == "logsig":
        o_ref[...] = _log_sigmoid(acc + refs[2][...])
    elif epi == "headnorm":
        g = refs[2][...]
        for j in range(acc.shape[1] // HD_B):
            blk = acc[:, j * HD_B:(j + 1) * HD_B]
            ms = jnp.mean(blk * blk, axis=-1, keepdims=True)
            o_ref[:, j * HD_B:(j + 1) * HD_B] = blk * lax.rsqrt(ms + EPS) * g
    elif epi == "resgate":
        gate = jnp.where(pl.program_id(1) < n_prompt_tiles, refs[3][...], refs[4][...])
        o_ref[...] = refs[2][...] + gate * acc
    else:
        o_ref[...] = acc


def _mm(x, w, *, ncols, tn, name, wl=None, col0=0, tm=ROW_TILE, epi="none", bias=None, bl=0,
        gain=None, xres=None, gate_tabs=None, gate_comp=0, geo=None):
    m_rows, k_dim = x.shape
    cb0 = col0 // tn
    grid = (ncols // tn, m_rows // tm)
    x_spec = pl.BlockSpec((tm, k_dim), lambda n, m: (m, 0))
    if wl is None:
        w_spec = pl.BlockSpec((k_dim, tn), lambda n, m: (0, cb0 + n))
    else:
        w_spec = pl.BlockSpec((None, k_dim, tn), lambda n, m: (wl, 0, cb0 + n))
    o_spec = pl.BlockSpec((tm, tn), lambda n, m: (m, n))
    args, specs = [x, w], [x_spec, w_spec]
    n_prompt_tiles = 0
    if epi in ("bias", "logsig"):
        args.append(bias)
        specs.append(pl.BlockSpec((None, 1, tn), lambda n, m: (bl, 0, cb0 + n)))
    elif epi == "headnorm":
        args.append(gain.reshape(1, HD_B))
        specs.append(pl.BlockSpec((1, HD_B), lambda n, m: (0, 0)))
    elif epi == "resgate":
        n_prompt_tiles = _n_prompt(geo) // ROW_TILE
        gp, gs = _mod_specs(gate_comp, geo, lambda n, m: m, lambda n, m: n, tn)
        args += [xres, gate_tabs[0], gate_tabs[1]]
        specs += [o_spec, gp, gs]
    return pl.pallas_call(
        functools.partial(_mm_kernel, epi=epi, n_prompt_tiles=n_prompt_tiles),
        out_shape=jax.ShapeDtypeStruct((m_rows, ncols), F32),
        grid=grid,
        in_specs=specs,
        out_specs=o_spec,
        compiler_params=_params(("arbitrary", "arbitrary")),
        name=name,
    )(*args)


def _gla_kernel(*refs, blk, n_chunks, has_s0):
    if has_s0:
        (q_ref, k_ref, v_ref, gl_ref, wg2_ref, bg2_ref, r_ref, on_ref, _obuf, s0_ref,
         o_ref, sout_ref, s_scr) = refs
    else:
        (q_ref, k_ref, v_ref, gl_ref, wg2_ref, bg2_ref, r_ref, on_ref, _obuf,
         o_ref, sout_ref, s_scr) = refs
    c = pl.program_id(1)

    @pl.when(c == 0)
    def _():
        if has_s0:
            s_scr[...] = s0_ref[...]
        else:
            s_scr[...] = jnp.zeros_like(s_scr)

    log_a = _log_sigmoid(jnp.dot(gl_ref[...], wg2_ref[...], preferred_element_type=F32)
                         + bg2_ref[...]) / GATE_NORMALIZER
    r_i = lax.broadcasted_iota(I32, (blk, blk), 0)
    c_i = lax.broadcasted_iota(I32, (blk, blk), 1)
    causal = c_i <= r_i
    tril = causal.astype(BF16)
    hi, mid, lo = _split3(log_a)
    tri_dot = lambda part: jnp.dot(tril, part, preferred_element_type=F32)
    g_all = tri_dot(hi) + tri_dot(mid) + tri_dot(lo)

    for h in range(H_A):
        kc = slice(h * DK_A, (h + 1) * DK_A)
        vc = slice(h * DV_A, (h + 1) * DV_A)
        g_cum = g_all[:, kc]
        g_last = g_cum[blk - 1:blk, :]
        k = k_ref[:, kc]
        v = v_ref[:, vc]
        s_prev = s_scr[h]
        q_dec = q_ref[:, kc] * (DK_A ** -0.5) * jnp.exp(g_cum)
        k_dec = k * jnp.exp(-g_cum)
        att = lax.dot_general(q_dec, k_dec, (((1,), (1,)), ((), ())), preferred_element_type=F32)
        att = jnp.where(causal, att, 0.0)
        o = (jnp.dot(att, v, preferred_element_type=F32)
             + jnp.dot(q_dec, s_prev, preferred_element_type=F32))
        k_end = k * jnp.exp(g_last - g_cum)
        decay_col = jnp.transpose(jnp.broadcast_to(jnp.exp(g_last), (LANES, DK_A)))[:, 0:1]
        s_new = s_prev * decay_col + lax.dot_general(
            k_end, v, (((0,), (0,)), ((), ())), preferred_element_type=F32)
        s_scr[h] = s_new
        ms = jnp.mean(o * o, axis=-1, keepdims=True)
        r = r_ref[:, vc]
        o_ref[:, vc] = o * lax.rsqrt(ms + EPS) * on_ref[...] * (r * jax.nn.sigmoid(r))

    @pl.when(c == n_chunks - 1)
    def _():
        sout_ref[...] = s_scr[...]


def _gla_scan(qkv, gl, r, layer, w_g2, b_g2, onorm, s0, o_buf, *, row0, nb, t_len, name):
    blk = min(CHUNK, t_len)
    n_chunks = t_len // blk
    rb0 = row0 // blk
    has_s0 = s0 is not None
    rows = lambda b, c: rb0 + b * n_chunks + c
    in_specs = [
        pl.BlockSpec((blk, DK_TOT), lambda b, c: (rows(b, c), 0)),
        pl.BlockSpec((blk, DK_TOT), lambda b, c: (rows(b, c), 1)),
        pl.BlockSpec((blk, DV_TOT), lambda b, c: (rows(b, c), 2 * DK_TOT // DV_TOT)),
        pl.BlockSpec((blk, GATE_RANK), lambda b, c: (rows(b, c), 0)),
        pl.BlockSpec((None, GATE_RANK, DK_TOT), lambda b, c: (layer, 0, 0)),
        pl.BlockSpec((None, 1, DK_TOT), lambda b, c: (layer, 0, 0)),
        pl.BlockSpec((blk, DV_TOT), lambda b, c: (rows(b, c), 0)),
        pl.BlockSpec((None, 1, DV_A), lambda b, c: (layer, 0, 0)),
        pl.BlockSpec(memory_space=pl.ANY),
    ]
    args = [qkv, qkv, qkv, gl, w_g2, b_g2.reshape(N_A, 1, DK_TOT), r,
            onorm.reshape(N_A, 1, DV_A), o_buf]
    if has_s0:
        in_specs.append(pl.BlockSpec((None, None, H_A, DK_A, DV_A),
                                     lambda b, c: (layer, b, 0, 0, 0)))
        args.append(s0)
    return pl.pallas_call(
        functools.partial(_gla_kernel, blk=blk, n_chunks=n_chunks, has_s0=has_s0),
        out_shape=(jax.ShapeDtypeStruct(o_buf.shape, F32),
                   jax.ShapeDtypeStruct((nb, H_A, DK_A, DV_A), F32)),
        grid=(nb, n_chunks),
        in_specs=in_specs,
        out_specs=(pl.BlockSpec((blk, DV_TOT), lambda b, c: (rows(b, c), 0)),
                   pl.BlockSpec((None, H_A, DK_A, DV_A), lambda b, c: (b, 0, 0, 0))),
        scratch_shapes=[pltpu.VMEM((H_A, DK_A, DV_A), F32)],
        input_output_aliases={8: 0},
        compiler_params=_params(("arbitrary", "arbitrary")),
        name=name,
    )(*args)


def _cumsum_kernel(x_ref, o_ref, carry_ref, *, tb):
    @pl.when(pl.program_id(1) == 0)
    def _():
        carry_ref[...] = jnp.zeros_like(carry_ref)

    r_i = lax.broadcasted_iota(I32, (tb, tb), 0)
    c_i = lax.broadcasted_iota(I32, (tb, tb), 1)
    triu = (r_i <= c_i).astype(BF16)
    hi, mid, lo = _split3(x_ref[...])
    tri_dot = lambda part: jnp.dot(part, triu, preferred_element_type=F32)
    y = tri_dot(hi) + tri_dot(mid) + tri_dot(lo) + carry_ref[:, 0:1]
    o_ref[...] = y
    carry_ref[...] = jnp.broadcast_to(y[:, tb - 1:tb], carry_ref.shape)


def _cumsum_lanes(x, tb, name):
    nb, nh, t_len = x.shape
    spec = pl.BlockSpec((None, nh, tb), lambda b, j: (b, 0, j))
    return pl.pallas_call(
        functools.partial(_cumsum_kernel, tb=tb),
        out_shape=jax.ShapeDtypeStruct(x.shape, F32),
        grid=(nb, t_len // tb),
        in_specs=[spec],
        out_specs=spec,
        scratch_shapes=[pltpu.VMEM((nh, LANES), F32)],
        compiler_params=_params(("arbitrary", "arbitrary")),
        name=name,
    )(x)


ATT_TILE = 256


def _fox_prompt_kernel(q_ref, k_ref, v_ref, cq_ref, ck_ref, _obuf, o_ref, m_ref, l_ref, acc_ref,
                       *, n_k):
    qi = pl.program_id(1)
    ki = pl.program_id(2)

    @pl.when(ki == 0)
    def _():
        m_ref[...] = jnp.full_like(m_ref, -jnp.inf)
        l_ref[...] = jnp.zeros_like(l_ref)
        acc_ref[...] = jnp.zeros_like(acc_ref)

    @pl.when(ki <= qi)
    def _():
        qpos = qi * ATT_TILE + lax.broadcasted_iota(I32, (ATT_TILE, ATT_TILE), 0)
        kpos = ki * ATT_TILE + lax.broadcasted_iota(I32, (ATT_TILE, ATT_TILE), 1)
        visible = kpos <= qpos
        for h in range(H_B):
            cols = slice(h * HD_B, (h + 1) * HD_B)
            s = lax.dot_general(q_ref[:, cols], k_ref[:, cols], (((1,), (1,)), ((), ())),
                                preferred_element_type=F32) * (HD_B ** -0.5)
            s = s + cq_ref[:, h:h + 1] - ck_ref[h:h + 1, :]
            s = jnp.where(visible, s, -jnp.inf)
            m_prev = m_ref[h]
            m_next = jnp.maximum(m_prev, jnp.max(s, axis=1, keepdims=True))
            alpha = jnp.exp(m_prev - m_next)
            p = jnp.exp(s - jnp.concatenate([m_next] * (ATT_TILE // LANES), axis=1))
            l_ref[h] = alpha * l_ref[h] + jnp.sum(p, axis=1, keepdims=True)
            acc_ref[:, cols] = acc_ref[:, cols] * alpha + jnp.dot(
                p, v_ref[:, cols], preferred_element_type=F32)
            m_ref[h] = m_next

    @pl.when(ki == n_k - 1)
    def _():
        for h in range(H_B):
            cols = slice(h * HD_B, (h + 1) * HD_B)
            o_ref[:, cols] = acc_ref[:, cols] / l_ref[h]


def _fox_prompt(q, k, v, cq, ck_t, o_buf, geo, name):
    n_q = geo.tp // ATT_TILE
    kv_blk = lambda b, qi, ki: (b * n_q + jnp.minimum(ki, qi), 0)
    q_blk = lambda b, qi, ki: (b * n_q + qi, 0)
    return pl.pallas_call(
        functools.partial(_fox_prompt_kernel, n_k=n_q),
        out_shape=jax.ShapeDtypeStruct(o_buf.shape, F32),
        grid=(geo.bp, n_q, n_q),
        in_specs=[pl.BlockSpec((ATT_TILE, D_MODEL), q_blk),
                  pl.BlockSpec((ATT_TILE, D_MODEL), kv_blk),
                  pl.BlockSpec((ATT_TILE, D_MODEL), kv_blk),
                  pl.BlockSpec((ATT_TILE, H_B), q_blk),
                  pl.BlockSpec((None, H_B, ATT_TILE),
                               lambda b, qi, ki: (b, 0, jnp.minimum(ki, qi))),
                  pl.BlockSpec(memory_space=pl.ANY)],
        out_specs=pl.BlockSpec((ATT_TILE, D_MODEL), q_blk),
        scratch_shapes=[pltpu.VMEM((H_B, ATT_TILE, LANES), F32),
                        pltpu.VMEM((H_B, ATT_TILE, LANES), F32),
                        pltpu.VMEM((ATT_TILE, D_MODEL), F32)],
        input_output_aliases={5: 0},
        compiler_params=_params(("arbitrary", "arbitrary", "arbitrary")),
        name=name,
    )(q, k, v, cq, ck_t, o_buf)


def _fox_sample_kernel(q_ref, kp_ref, vp_ref, kn_ref, vn_ref, cq_ref, ckp_ref, ckn_ref, _oprev,
                       o_ref):
    t_len = q_ref.shape[0]
    scale = HD_B ** -0.5
    nt_dims = (((1,), (1,)), ((), ()))
    r_i = lax.broadcasted_iota(I32, (t_len, t_len), 0)
    c_i = lax.broadcasted_iota(I32, (t_len, t_len), 1)
    causal = c_i <= r_i
    for h in range(H_B):
        cols = slice(h * HD_B, (h + 1) * HD_B)
        q = q_ref[:, cols]
        cq = cq_ref[:, h:h + 1]
        s_past = lax.dot_general(q, kp_ref[:, cols], nt_dims, preferred_element_type=F32) * scale
        s_past = s_past + cq - ckp_ref[h:h + 1, :]
        s_new = lax.dot_general(q, kn_ref[:, cols], nt_dims, preferred_element_type=F32) * scale
        s_new = jnp.where(causal, s_new + cq - ckn_ref[h:h + 1, :], -jnp.inf)
        m = jnp.maximum(jnp.max(s_past, axis=1, keepdims=True),
                        jnp.max(s_new, axis=1, keepdims=True))
        p_past = jnp.exp(s_past - m)
        p_new = jnp.exp(s_new - m)
        denom = jnp.sum(p_past, axis=1, keepdims=True) + jnp.sum(p_new, axis=1, keepdims=True)
        o = (jnp.dot(p_past, vp_ref[:, cols], preferred_element_type=F32)
             + jnp.dot(p_new, vn_ref[:, cols], preferred_element_type=F32))
        o_ref[:, cols] = o / denom


def _fox_sample(q, cache_k, cache_v, k_new, v_new, cq, ck_past, ck_new, o_prev, geo, name):
    rb0 = _n_prompt(geo) // geo.ts
    new_blk = pl.BlockSpec((geo.ts, D_MODEL), lambda b: (rb0 + b, 0))
    past_blk = pl.BlockSpec((None, geo.past, D_MODEL), lambda b: (b, 0, 0))
    return pl.pallas_call(
        _fox_sample_kernel,
        out_shape=jax.ShapeDtypeStruct(o_prev.shape, F32),
        grid=(geo.bs,),
        in_specs=[new_blk, past_blk, past_blk, new_blk, new_blk,
                  pl.BlockSpec((None, geo.ts, H_B), lambda b: (b, 0, 0)),
                  pl.BlockSpec((None, H_B, geo.past), lambda b: (b, 0, 0)),
                  pl.BlockSpec((None, H_B, geo.ts), lambda b: (b, 0, 0)),
                  pl.BlockSpec(memory_space=pl.ANY)],
        out_specs=new_blk,
        input_output_aliases={8: 0},
        compiler_params=_params(("arbitrary",)),
        name=name,
    )(q, cache_k, cache_v, k_new, v_new, cq, ck_past, ck_new, o_prev)


def _max_tiles(n_rows):
    return -(-(n_rows * TOP_K + N_EXPERTS * (ROW_TILE - 1)) // ROW_TILE)


def _route_plan(top_i, rank, counts, n_rows):
    counts = counts.reshape(N_EXPERTS).astype(I32)
    padded = ((counts + ROW_TILE - 1) // ROW_TILE) * ROW_TILE
    g_end = jnp.cumsum(padded)
    g_start = g_end - padded
    onehot = top_i[..., None] == jnp.arange(N_EXPERTS, dtype=I32)
    pos = jnp.sum(jnp.where(onehot, g_start, 0), axis=-1) + rank
    n_tiles = g_end[-1] // ROW_TILE
    max_tiles = _max_tiles(n_rows)
    tile_ids = jnp.arange(max_tiles, dtype=I32)
    tile_e = jnp.sum(((tile_ids * ROW_TILE)[:, None] >= g_end[None, :]).astype(I32), axis=1)
    last_e = jnp.sum((((n_tiles - 1) * ROW_TILE) >= g_end).astype(I32))
    tile_e = jnp.where(tile_ids < n_tiles, tile_e, last_e).astype(I32)
    return (pos.reshape(-1).astype(I32), tile_e, n_tiles.reshape(1).astype(I32),
            (g_start + counts).astype(I32), (padded - counts).astype(I32))


DRAIN_UNROLL = 16


def _token_slab(ref, token):
    row0 = token * SLAB_ROWS
    if not isinstance(row0, int):
        row0 = pl.multiple_of(row0, SLAB_ROWS)
    return ref.at[:, pl.ds(row0, SLAB_ROWS), :]


def _token_copy(src, src_token, dst, dst_token, sem):
    return pltpu.make_async_copy(_token_slab(src, src_token), _token_slab(dst, dst_token), sem)


def _dispatch_kernel(pos_ref, pad0_ref, padn_ref, h_ref, xs_ref, zero_ref, sem):
    i = pl.program_id(0)

    @pl.when(i == 0)
    def _():
        zero_ref[...] = jnp.zeros_like(zero_ref)

        def per_expert(e, carry):
            start = pad0_ref[e]
            count = padn_ref[e]

            def issue(j, c):
                _token_copy(zero_ref, 0, xs_ref, start + j, sem).start()
                return c

            def drain(j, c):
                _token_copy(zero_ref, 0, xs_ref, start + j, sem).wait()
                return c

            lax.fori_loop(0, count, issue, 0)
            lax.fori_loop(0, count, drain, 0)
            return carry

        lax.fori_loop(0, N_EXPERTS, per_expert, 0)

    def issue_row(r, c):
        for k in range(TOP_K):
            p = pos_ref[(i * ROW_TILE + r) * TOP_K + k]
            _token_copy(h_ref, r, xs_ref, p, sem).start()
        return c

    def drain_rows(g, c):
        for _ in range(DRAIN_UNROLL * TOP_K):
            _token_copy(h_ref, 0, xs_ref, 0, sem).wait()
        return c

    lax.fori_loop(0, ROW_TILE, issue_row, 0)
    lax.fori_loop(0, ROW_TILE // DRAIN_UNROLL, drain_rows, 0)


def _dispatch(h_tm, pos, pad0, padn, layer):
    n = h_tm.shape[1] // SLAB_ROWS
    rows_sorted = _max_tiles(n) * ROW_TILE
    grid_spec = pltpu.PrefetchScalarGridSpec(
        num_scalar_prefetch=3,
        grid=(n // ROW_TILE,),
        in_specs=[pl.BlockSpec(_tm_shape(ROW_TILE), lambda i, *_: (0, i, 0))],
        out_specs=pl.BlockSpec(memory_space=pl.ANY),
        scratch_shapes=[pltpu.VMEM(_tm_shape(1), F32), pltpu.SemaphoreType.DMA],
    )
    return pl.pallas_call(
        _dispatch_kernel,
        out_shape=jax.ShapeDtypeStruct(_tm_shape(rows_sorted), F32),
        grid_spec=grid_spec,
        compiler_params=_params(("arbitrary",)),
        name=f"moe_dispatch_{layer}",
    )(pos, pad0, padn, h_tm)


FF_TILE = 1024


def _moe_up_kernel(te_ref, nt_ref, xs_ref, wg_ref, wu_ref, bg_ref, bu_ref, a_ref):
    @pl.when(pl.program_id(1) < nt_ref[0])
    def _():
        x = _load_token_major(xs_ref, ROW_TILE)
        g = jnp.dot(x, wg_ref[...], preferred_element_type=F32) + bg_ref[...]
        u = jnp.dot(x, wu_ref[...], preferred_element_type=F32) + bu_ref[...]
        g = jnp.minimum(g, SWIGLU_LIMIT)
        u = jnp.clip(u, -SWIGLU_LIMIT, SWIGLU_LIMIT)
        a_ref[...] = g * jax.nn.sigmoid(SWIGLU_ALPHA * g) * (u + 1.0)


def _moe_down_kernel(te_ref, nt_ref, a_ref, w_ref, b_ref, y_ref):
    @pl.when(pl.program_id(1) < nt_ref[0])
    def _():
        y = jnp.dot(a_ref[...], w_ref[...], preferred_element_type=F32) + b_ref[...]
        _store_token_major(y_ref, y)


def _moe_experts(xs_tm, tile_e, n_tiles, w_in, b_in, w_out, b_out, layer):
    rows_sorted = xs_tm.shape[1] // SLAB_ROWS
    max_tiles = rows_sorted // ROW_TILE
    u_blk0 = D_FF // FF_TILE
    row_blk = lambda n, r, te, nt: (jnp.minimum(r, nt[0] - 1), 0)
    tm_spec = pl.BlockSpec(_tm_shape(ROW_TILE),
                           lambda n, r, te, nt: (0, jnp.minimum(r, nt[0] - 1), 0))
    up_spec = pltpu.PrefetchScalarGridSpec(
        num_scalar_prefetch=2,
        grid=(D_FF // FF_TILE, max_tiles),
        in_specs=[
            tm_spec,
            pl.BlockSpec((None, None, D_MODEL, FF_TILE), lambda n, r, te, nt: (layer, te[r], 0, n)),
            pl.BlockSpec((None, None, D_MODEL, FF_TILE),
                         lambda n, r, te, nt: (layer, te[r], 0, u_blk0 + n)),
            pl.BlockSpec((None, None, 1, FF_TILE), lambda n, r, te, nt: (layer, te[r], 0, n)),
            pl.BlockSpec((None, None, 1, FF_TILE),
                         lambda n, r, te, nt: (layer, te[r], 0, u_blk0 + n)),
        ],
        out_specs=pl.BlockSpec((ROW_TILE, FF_TILE),
                               lambda n, r, te, nt: (jnp.minimum(r, nt[0] - 1), n)),
    )
    b_in4 = b_in.reshape(DEPTH, N_EXPERTS, 1, 2 * D_FF)
    act = pl.pallas_call(
        _moe_up_kernel,
        out_shape=jax.ShapeDtypeStruct((rows_sorted, D_FF), F32),
        grid_spec=up_spec,
        compiler_params=_params(("arbitrary", "arbitrary")),
        name=f"moe_up_{layer}",
    )(tile_e, n_tiles, xs_tm, w_in, w_in, b_in4, b_in4)
    down_spec = pltpu.PrefetchScalarGridSpec(
        num_scalar_prefetch=2,
        grid=(1, max_tiles),
        in_specs=[
            pl.BlockSpec((ROW_TILE, D_FF), row_blk),
            pl.BlockSpec((None, None, D_FF, D_MODEL), lambda n, r, te, nt: (layer, te[r], 0, 0)),
            pl.BlockSpec((None, None, 1, D_MODEL), lambda n, r, te, nt: (layer, te[r], 0, 0)),
        ],
        out_specs=tm_spec,
    )
    return pl.pallas_call(
        _moe_down_kernel,
        out_shape=jax.ShapeDtypeStruct(_tm_shape(rows_sorted), F32),
        grid_spec=down_spec,
        compiler_params=_params(("arbitrary", "arbitrary")),
        name=f"moe_down_{layer}",
    )(tile_e, n_tiles, act, w_out, b_out.reshape(DEPTH, N_EXPERTS, 1, D_MODEL))


def _combine_kernel(pos_ref, ys_ref, tw_ref, x_ref, gp_ref, gs_ref, o_ref, buf_ref, wexp_ref, sem,
                    *, n_prompt_tiles):
    i = pl.program_id(0)

    def issue_row(r, c):
        for k in range(TOP_K):
            p = pos_ref[(i * ROW_TILE + r) * TOP_K + k]
            _token_copy(ys_ref, p, buf_ref.at[k], r, sem).start()
        return c

    def drain_rows(g, c):
        for _ in range(DRAIN_UNROLL * TOP_K):
            _token_copy(ys_ref, 0, buf_ref.at[0], 0, sem).wait()
        return c

    lax.fori_loop(0, ROW_TILE, issue_row, 0)
    lax.fori_loop(0, ROW_TILE // DRAIN_UNROLL, drain_rows, 0)
    tw = tw_ref[...]
    for j in range(SLAB_ROWS):
        wexp_ref[pl.ds(j, ROW_TILE, stride=SLAB_ROWS), :] = tw
    for c in range(SLAB_ROWS):
        rows = pl.ds(c * ROW_TILE, ROW_TILE)
        w_cols = [wexp_ref[rows, k:k + 1] for k in range(TOP_K)]
        for p in range(SLAB_PARTS):
            y_tm = w_cols[0] * buf_ref[0, p, rows, :]
            for k in range(1, TOP_K):
                y_tm = y_tm + w_cols[k] * buf_ref[k, p, rows, :]
            buf_ref[0, p, rows, :] = y_tm
    y = _load_token_major(buf_ref.at[0], ROW_TILE)
    gate = jnp.where(i < n_prompt_tiles, gp_ref[...], gs_ref[...])
    o_ref[...] = x_ref[...] + gate * y


def _combine(ys, pos, top_w, x, gate_tabs, gate_comp, geo, layer):
    n = x.shape[0]
    gp, gs = _mod_specs(gate_comp, geo, lambda i, *_: i, lambda i, *_: 0, D_MODEL)
    x_spec = pl.BlockSpec((ROW_TILE, D_MODEL), lambda i, *_: (i, 0))
    grid_spec = pltpu.PrefetchScalarGridSpec(
        num_scalar_prefetch=1,
        grid=(n // ROW_TILE,),
        in_specs=[pl.BlockSpec(memory_space=pl.ANY),
                  pl.BlockSpec((ROW_TILE, LANES), lambda i, *_: (i, 0)),
                  x_spec, gp, gs],
        out_specs=x_spec,
        scratch_shapes=[pltpu.VMEM((TOP_K,) + _tm_shape(ROW_TILE), F32),
                        pltpu.VMEM((ROW_TILE * SLAB_ROWS, LANES), F32),
                        pltpu.SemaphoreType.DMA],
    )
    return pl.pallas_call(
        functools.partial(_combine_kernel, n_prompt_tiles=_n_prompt(geo) // ROW_TILE),
        out_shape=jax.ShapeDtypeStruct((n, D_MODEL), F32),
        grid_spec=grid_spec,
        compiler_params=_params(("arbitrary",)),
        name=f"moe_combine_{layer}",
    )(pos, ys, top_w, x, gate_tabs[0], gate_tabs[1])


def _trunk(x, cs, state_gla, cache_k, cache_v, cache_logf, geo,
           w_mod, b_mod, norm_mix, norm_ffn,
           w_gla_in, w_gla_g2, b_gla_g2, gla_onorm, w_gla_out,
           w_kv_mod, b_kv_mod, norm_kv, w_kv, b_f, k_norm,
           w_fox_q, q_norm, w_fox_out,
           w_router, b_router, w_moe_in, b_moe_in, w_moe_out, b_moe_out):
    n = _n_rows(geo)
    n_p = _n_prompt(geo)
    b_mod3 = b_mod.reshape(DEPTH, 1, 6 * D_MODEL)
    gla_p, gla_s = [], []
    kv = None
    for layer in range(DEPTH):
        if layer == N_A:
            kv_mod = _mm(cs, w_kv_mod, ncols=2 * D_MODEL, tn=1024, tm=MOD_ROWS, epi="bias",
                         bias=b_kv_mod.reshape(1, 1, 2 * D_MODEL), name="kv_mod")
            kv_tabs = _mod_tables(kv_mod, 2, geo)
            hk = _norm_mod(x, norm_kv, kv_tabs, 0, 1, geo, "norm_kv")
            k_new = _mm(hk, w_kv, ncols=D_MODEL, tn=1024, epi="headnorm", gain=k_norm, name="kv_k")
            v_new = _mm(hk, w_kv, ncols=D_MODEL, tn=1024, col0=D_MODEL, name="kv_v")
            logf = _mm(hk, w_kv[:, 2 * D_MODEL:], ncols=H_B, tn=H_B, epi="logsig",
                       bias=b_f.reshape(1, 1, H_B), name="kv_f")
            logf_p = logf[:n_p].reshape(geo.bp, geo.tp, H_B)
            logf_s = logf[n_p:].reshape(geo.bs, geo.ts, H_B)
            c_p_t = _cumsum_lanes(logf_p.transpose(0, 2, 1), 512, "cumsum_p")
            cq_p = c_p_t.transpose(0, 2, 1).reshape(n_p, H_B)
            all_s = jnp.concatenate([cache_logf.astype(F32), logf_s], axis=1).transpose(0, 2, 1)
            c_s_t = _cumsum_lanes(all_s, geo.past + geo.ts, "cumsum_s")
            ck_past = c_s_t[:, :, :geo.past]
            ck_new = c_s_t[:, :, geo.past:]
            cq_s = ck_new.transpose(0, 2, 1)
            cache_k2 = cache_k.reshape(geo.bs, geo.past, D_MODEL)
            cache_v2 = cache_v.reshape(geo.bs, geo.past, D_MODEL)
            kv = (k_new, v_new, logf_p, logf_s)

        mod = _mm(cs, w_mod, wl=layer, ncols=6 * D_MODEL, tn=1024, tm=MOD_ROWS, epi="bias",
                  bias=b_mod3, bl=layer, name=f"mod_{layer}")
        tabs = _mod_tables(mod, 6, geo)
        h = _norm_mod(x, norm_mix[layer], tabs, 0, 1, geo, f"norm_mix_{layer}")
        if layer < N_A:
            qkv = _mm(h, w_gla_in, wl=layer, ncols=2 * DK_TOT + DV_TOT, tn=1024, name=f"gla_qkv_{layer}")
            gcol = 2 * DK_TOT + DV_TOT
            gl = _mm(h, w_gla_in[layer, :, gcol:gcol + GATE_RANK], ncols=GATE_RANK, tn=GATE_RANK,
                     name=f"gla_gl_{layer}")
            r = _mm(h, w_gla_in[layer, :, gcol + GATE_RANK:], ncols=DV_TOT, tn=1024,
                    name=f"gla_r_{layer}")
            o, s_p = _gla_scan(qkv, gl, r, layer, w_gla_g2, b_gla_g2, gla_onorm, None, h,
                               row0=0, nb=geo.bp, t_len=geo.tp, name=f"gla_scan_p_{layer}")
            o, s_s = _gla_scan(qkv, gl, r, layer, w_gla_g2, b_gla_g2, gla_onorm, state_gla, o,
                               row0=n_p, nb=geo.bs, t_len=geo.ts, name=f"gla_scan_s_{layer}")
            gla_p.append(s_p)
            gla_s.append(s_s)
            x = _mm(o, w_gla_out, wl=layer, ncols=D_MODEL, tn=1024, epi="resgate", xres=x,
                    gate_tabs=tabs, gate_comp=2, geo=geo, name=f"gla_out_{layer}")
        else:
            j = layer - N_A
            q = _mm(h, w_fox_q, wl=j, ncols=D_MODEL, tn=1024, epi="headnorm", gain=q_norm[j],
                    name=f"fox_q_{j}")
            o = _fox_prompt(q, kv[0], kv[1], cq_p, c_p_t, h, geo, f"fox_attn_p_{j}")
            o = _fox_sample(q, cache_k2, cache_v2, kv[0], kv[1], cq_s, ck_past, ck_new, o, geo,
                            f"fox_attn_s_{j}")
            x = _mm(o, w_fox_out, wl=j, ncols=D_MODEL, tn=1024, epi="resgate", xres=x,
                    gate_tabs=tabs, gate_comp=2, geo=geo, name=f"fox_out_{j}")

        h2, top_i, top_w, rank, counts = _norm_router(x, norm_ffn[layer], tabs, 3, 4,
                                                      w_router, b_router, layer, geo)
        pos, tile_e, n_tiles, pad0, padn = _route_plan(top_i[:, :TOP_K], rank[:, :TOP_K], counts, n)
        xs = _dispatch(h2, pos, pad0, padn, layer)
        ys = _moe_experts(xs, tile_e, n_tiles, w_moe_in, b_moe_in, w_moe_out, b_moe_out, layer)
        x = _combine(ys, pos, top_w, x, tabs, 5, geo, layer)
    return x, gla_p, gla_s, kv


def kernel(x_prompt, x_sample, c_prompt, c_sample, state_gla, cache_k, cache_v, cache_logf, w_mod, b_mod, norm_mix, norm_ffn, w_gla_in, w_gla_g2, b_gla_g2, gla_onorm, w_gla_out, w_kv_mod, b_kv_mod, norm_kv, w_kv, b_f, k_norm, w_fox_q, q_norm, w_fox_out, w_router, b_router, w_moe_in, b_moe_in, w_moe_out, b_moe_out):
    bp, tp, _ = x_prompt.shape
    bs, ts, _ = x_sample.shape
    geo = Geo(bp, tp, bs, ts, cache_k.shape[1])
    assert bs * ts == ROW_TILE and tp % ROW_TILE == 0 and tp % 512 == 0 and bp + bs <= MOD_ROWS
    n_p = bp * tp
    x = jnp.concatenate([x_prompt.reshape(n_p, D_MODEL), x_sample.reshape(bs * ts, D_MODEL)], axis=0)
    c = jnp.concatenate([c_prompt, c_sample], axis=0).astype(F32)
    cs = jnp.pad(c * jax.nn.sigmoid(c), ((0, MOD_ROWS - bp - bs), (0, 0)))
    x, gla_p, gla_s, (k_new, v_new, logf_p, logf_s) = _trunk(
        x, cs, state_gla, cache_k, cache_v, cache_logf, geo,
        w_mod, b_mod, norm_mix, norm_ffn,
        w_gla_in, w_gla_g2, b_gla_g2, gla_onorm, w_gla_out,
        w_kv_mod, b_kv_mod, norm_kv, w_kv, b_f, k_norm,
        w_fox_q, q_norm, w_fox_out,
        w_router, b_router, w_moe_in, b_moe_in, w_moe_out, b_moe_out)
    return (x[:n_p].reshape(bp, tp, D_MODEL),
            x[n_p:].reshape(bs, ts, D_MODEL),
            jnp.stack(gla_p), jnp.stack(gla_s),
            k_new[:n_p].reshape(bp, tp, H_B, HD_B),
            v_new[:n_p].reshape(bp, tp, H_B, HD_B),
            logf_p,
            k_new[n_p:].reshape(bs, ts, H_B, HD_B),
            v_new[n_p:].reshape(bs, ts, H_B, HD_B),
            logf_s)
```

```python
import collections
import functools

import jax
import jax.numpy as jnp
from jax import lax
from jax.experimental import pallas as pl
from jax.experimental.pallas import tpu as pltpu

F32 = jnp.float32
BF16 = jnp.bfloat16
I32 = jnp.int32

D_MODEL = 2048
DEPTH = 4
N_A = 2
H_A = 4
DK_A = 256
DV_A = 512
DK_TOT = H_A * DK_A
DV_TOT = H_A * DV_A
GATE_RANK = 16
GATE_NORMALIZER = 16.0
H_B = 16
HD_B = 128
N_EXPERTS = 32
TOP_K = 4
D_FF = D_MODEL
SWIGLU_LIMIT = 7.0
SWIGLU_ALPHA = 1.702
EPS = 1e-6
CHUNK = 64

ROW_TILE = 256
LANES = 128
SUBLANES = 8
SLAB_ROWS = SUBLANES
SLAB_PARTS = D_MODEL // (SLAB_ROWS * LANES)


def _tm_shape(rows):
    return (SLAB_PARTS, rows * SLAB_ROWS, LANES)
MOD_ROWS = 24
VMEM_LIMIT_BYTES = 48 * 1024 * 1024

Geo = collections.namedtuple("Geo", "bp tp bs ts past")


def _n_prompt(geo):
    return geo.bp * geo.tp


def _n_rows(geo):
    return geo.bp * geo.tp + geo.bs * geo.ts


def _params(sem):
    return pltpu.CompilerParams(dimension_semantics=sem, vmem_limit_bytes=VMEM_LIMIT_BYTES)


def _log_sigmoid(x):
    return jnp.minimum(x, 0.0) - jnp.log1p(jnp.exp(-jnp.abs(x)))


def _store_token_major(ref, val):
    rows = val.shape[0]
    for j in range(SLAB_ROWS):
        for p in range(SLAB_PARTS):
            g = j * SLAB_PARTS + p
            ref[p, pl.ds(j, rows, stride=SLAB_ROWS), :] = val[:, g * LANES:(g + 1) * LANES]


def _load_token_major(ref, rows):
    return jnp.concatenate(
        [ref[p, pl.ds(j, rows, stride=SLAB_ROWS), :]
         for j in range(SLAB_ROWS) for p in range(SLAB_PARTS)], axis=1)


def _split3(x):
    hi = x.astype(BF16)
    r1 = x - hi.astype(F32)
    mid = r1.astype(BF16)
    lo = (r1 - mid.astype(F32)).astype(BF16)
    return hi, mid, lo


def _mod_tables(mod, ncomp, geo):
    m = mod.reshape(mod.shape[0], ncomp, D_MODEL)
    tab_p = m[:geo.bp].transpose(1, 0, 2).reshape(ncomp, geo.bp, 1, D_MODEL)
    tab_s = jnp.repeat(m[geo.bp:geo.bp + geo.bs], geo.ts, axis=0).transpose(1, 0, 2)
    return tab_p, tab_s


def _mod_specs(comp, geo, tile_of, col_of, width):
    tiles_per_batch = geo.tp // ROW_TILE

    def idx_p(*ids):
        return (comp, jnp.minimum(tile_of(*ids) // tiles_per_batch, geo.bp - 1), 0, col_of(*ids))

    def idx_s(*ids):
        return (comp, 0, col_of(*ids))

    return (pl.BlockSpec((None, None, 1, width), idx_p),
            pl.BlockSpec((None, ROW_TILE, width), idx_s))


def _norm_mod_tile(x, g, sh_p, sc_p, sh_s, sc_s, is_prompt):
    ms = jnp.mean(x * x, axis=-1, keepdims=True)
    xn = x * lax.rsqrt(ms + EPS) * g
    shift = jnp.where(is_prompt, sh_p, sh_s)
    scale = jnp.where(is_prompt, sc_p, sc_s)
    return xn * (1.0 + scale) + shift


def _norm_mod_kernel(x_ref, g_ref, shp_ref, scp_ref, shs_ref, scs_ref, h_ref, *, n_prompt_tiles):
    is_prompt = pl.program_id(0) < n_prompt_tiles
    h_ref[...] = _norm_mod_tile(x_ref[...], g_ref[...], shp_ref[...], scp_ref[...],
                                shs_ref[...], scs_ref[...], is_prompt)


def _norm_mod(x, g, tabs, comp_shift, comp_scale, geo, name):
    n = x.shape[0]
    tab_p, tab_s = tabs
    row = lambda i: i
    col = lambda i: 0
    shp, shs = _mod_specs(comp_shift, geo, row, col, D_MODEL)
    scp, scs = _mod_specs(comp_scale, geo, row, col, D_MODEL)
    x_spec = pl.BlockSpec((ROW_TILE, D_MODEL), lambda i: (i, 0))
    return pl.pallas_call(
        functools.partial(_norm_mod_kernel, n_prompt_tiles=_n_prompt(geo) // ROW_TILE),
        out_shape=jax.ShapeDtypeStruct((n, D_MODEL), F32),
        grid=(n // ROW_TILE,),
        in_specs=[x_spec, pl.BlockSpec((1, D_MODEL), lambda i: (0, 0)), shp, scp, shs, scs],
        out_specs=x_spec,
        compiler_params=_params(("arbitrary",)),
        name=name,
    )(x, g.reshape(1, D_MODEL), tab_p, tab_p, tab_s, tab_s)


def _norm_router_kernel(x_ref, g_ref, shp_ref, scp_ref, shs_ref, scs_ref, wr_ref, br_ref,
                        h_ref, ti_ref, tw_ref, rk_ref, cnt_ref, run_ref, *, n_prompt_tiles):
    i = pl.program_id(0)

    @pl.when(i == 0)
    def _():
        run_ref[...] = jnp.zeros_like(run_ref)

    h = _norm_mod_tile(x_ref[...], g_ref[...], shp_ref[...], scp_ref[...],
                       shs_ref[...], scs_ref[...], i < n_prompt_tiles)
    _store_token_major(h_ref, h)
    logits = jnp.dot(h, wr_ref[...], preferred_element_type=F32) + br_ref[...]
    lane = lax.broadcasted_iota(I32, logits.shape, 1)
    vals, idxs = [], []
    cur = logits
    for _ in range(TOP_K):
        m = jnp.max(cur, axis=-1, keepdims=True)
        idx = jnp.min(jnp.where(cur == m, lane, N_EXPERTS), axis=-1, keepdims=True)
        vals.append(m)
        idxs.append(idx)
        cur = jnp.where(lane == idx, -jnp.inf, cur)
    exps = [jnp.exp(v - vals[0]) for v in vals]
    denom = exps[0] + exps[1] + exps[2] + exps[3]

    r_i = lax.broadcasted_iota(I32, (ROW_TILE, ROW_TILE), 0)
    c_i = lax.broadcasted_iota(I32, (ROW_TILE, ROW_TILE), 1)
    tril_excl = (c_i < r_i).astype(F32)
    base = run_ref[...]
    ranks = []
    for k in range(TOP_K):
        onehot = (lane == idxs[k]).astype(F32)
        prefix = jnp.dot(tril_excl, onehot, preferred_element_type=F32)
        ranks.append(jnp.sum(onehot * (prefix + base), axis=-1, keepdims=True))
        base = base + jnp.sum(onehot, axis=0, keepdims=True)
    run_ref[...] = base
    cnt_ref[...] = base

    lane_out = lax.broadcasted_iota(I32, (ROW_TILE, LANES), 1)
    ti = jnp.zeros((ROW_TILE, LANES), I32)
    tw = jnp.zeros((ROW_TILE, LANES), F32)
    rk = jnp.zeros((ROW_TILE, LANES), I32)
    for k in range(TOP_K):
        sel = lane_out == k
        ti = jnp.where(sel, idxs[k], ti)
        tw = jnp.where(sel, exps[k] / denom, tw)
        rk = jnp.where(sel, ranks[k].astype(I32), rk)
    ti_ref[...] = ti
    tw_ref[...] = tw
    rk_ref[...] = rk


def _norm_router(x, g, tabs, comp_shift, comp_scale, w_router, b_router, layer, geo):
    n = x.shape[0]
    tab_p, tab_s = tabs
    row = lambda i: i
    col = lambda i: 0
    shp, shs = _mod_specs(comp_shift, geo, row, col, D_MODEL)
    scp, scs = _mod_specs(comp_scale, geo, row, col, D_MODEL)
    x_spec = pl.BlockSpec((ROW_TILE, D_MODEL), lambda i: (i, 0))
    lane_spec = pl.BlockSpec((ROW_TILE, LANES), lambda i: (i, 0))
    tm_spec = pl.BlockSpec(_tm_shape(ROW_TILE), lambda i: (0, i, 0))
    return pl.pallas_call(
        functools.partial(_norm_router_kernel, n_prompt_tiles=_n_prompt(geo) // ROW_TILE),
        out_shape=(jax.ShapeDtypeStruct(_tm_shape(n), F32),
                   jax.ShapeDtypeStruct((n, LANES), I32),
                   jax.ShapeDtypeStruct((n, LANES), F32),
                   jax.ShapeDtypeStruct((n, LANES), I32),
                   jax.ShapeDtypeStruct((1, N_EXPERTS), F32)),
        grid=(n // ROW_TILE,),
        in_specs=[x_spec, pl.BlockSpec((1, D_MODEL), lambda i: (0, 0)), shp, scp, shs, scs,
                  pl.BlockSpec((None, D_MODEL, N_EXPERTS), lambda i: (layer, 0, 0)),
                  pl.BlockSpec((None, 1, N_EXPERTS), lambda i: (layer, 0, 0))],
        out_specs=(tm_spec, lane_spec, lane_spec, lane_spec,
                   pl.BlockSpec((1, N_EXPERTS), lambda i: (0, 0))),
        scratch_shapes=[pltpu.VMEM((1, N_EXPERTS), F32)],
        compiler_params=_params(("arbitrary",)),
        name=f"norm_router_{layer}",
    )(x, g.reshape(1, D_MODEL), tab_p, tab_p, tab_s, tab_s, w_router,
      b_router.reshape(DEPTH, 1, N_EXPERTS))


def _mm_kernel(*refs, epi, n_prompt_tiles):
    x_ref, w_ref = refs[0], refs[1]
    o_ref = refs[-1]
    acc = jnp.dot(x_ref[...], w_ref[...], preferred_element_type=F32)
    if epi == "bias":
        o_ref[...] = acc + refs[2][...]
    elif epi == "logsig":
        o_ref[...] = _log_sigmoid(acc + refs[2][...])
    elif epi == "headnorm":
        g = refs[2][...]
        for j in range(acc.shape[1] // HD_B):
            blk = acc[:, j * HD_B:(j + 1) * HD_B]
            ms = jnp.mean(blk * blk, axis=-1, keepdims=True)
            o_ref[:, j * HD_B:(j + 1) * HD_B] = blk * lax.rsqrt(ms + EPS) * g
    elif epi == "resgate":
        gate = jnp.where(pl.program_id(1) < n_prompt_tiles, refs[3][...], refs[4][...])
        o_ref[...] = refs[2][...] + gate * acc
    else:
        o_ref[...] = acc


def _dense_row_tile(m_rows):
    return max(t for t in (ROW_TILE, 2 * ROW_TILE, 3 * ROW_TILE) if m_rows % t == 0)


def _mm(x, w, *, ncols, tn, name, wl=None, col0=0, tm=None, epi="none", bias=None, bl=0,
        gain=None, xres=None, gate_tabs=None, gate_comp=0, geo=None):
    m_rows, k_dim = x.shape
    if tm is None:
        tm = ROW_TILE if epi == "resgate" else _dense_row_tile(m_rows)
    cb0 = col0 // tn
    grid = (ncols // tn, m_rows // tm)
    x_spec = pl.BlockSpec((tm, k_dim), lambda n, m: (m, 0))
    if wl is None:
        w_spec = pl.BlockSpec((k_dim, tn), lambda n, m: (0, cb0 + n))
    else:
        w_spec = pl.BlockSpec((None, k_dim, tn), lambda n, m: (wl, 0, cb0 + n))
    o_spec = pl.BlockSpec((tm, tn), lambda n, m: (m, n))
    args, specs = [x, w], [x_spec, w_spec]
    n_prompt_tiles = 0
    if epi in ("bias", "logsig"):
        args.append(bias)
        specs.append(pl.BlockSpec((None, 1, tn), lambda n, m: (bl, 0, cb0 + n)))
    elif epi == "headnorm":
        args.append(gain.reshape(1, HD_B))
        specs.append(pl.BlockSpec((1, HD_B), lambda n, m: (0, 0)))
    elif epi == "resgate":
        n_prompt_tiles = _n_prompt(geo) // ROW_TILE
        gp, gs = _mod_specs(gate_comp, geo, lambda n, m: m, lambda n, m: n, tn)
        args += [xres, gate_tabs[0], gate_tabs[1]]
        specs += [o_spec, gp, gs]
    return pl.pallas_call(
        functools.partial(_mm_kernel, epi=epi, n_prompt_tiles=n_prompt_tiles),
        out_shape=jax.ShapeDtypeStruct((m_rows, ncols), F32),
        grid=grid,
        in_specs=specs,
        out_specs=o_spec,
        compiler_params=_params(("arbitrary", "arbitrary")),
        name=name,
    )(*args)


def _gla_kernel(*refs, blk, n_chunks, has_s0):
    if has_s0:
        (q_ref, k_ref, v_ref, gl_ref, wg2_ref, bg2_ref, r_ref, on_ref, _obuf, s0_ref,
         o_ref, sout_ref, s_scr) = refs
    else:
        (q_ref, k_ref, v_ref, gl_ref, wg2_ref, bg2_ref, r_ref, on_ref, _obuf,
         o_ref, sout_ref, s_scr) = refs
    c = pl.program_id(1)

    @pl.when(c == 0)
    def _():
        if has_s0:
            s_scr[...] = s0_ref[...]
        else:
            s_scr[...] = jnp.zeros_like(s_scr)

    log_a = _log_sigmoid(jnp.dot(gl_ref[...], wg2_ref[...], preferred_element_type=F32)
                         + bg2_ref[...]) / GATE_NORMALIZER
    r_i = lax.broadcasted_iota(I32, (blk, blk), 0)
    c_i = lax.broadcasted_iota(I32, (blk, blk), 1)
    causal = c_i <= r_i
    tril = causal.astype(BF16)
    hi, mid, lo = _split3(log_a)
    tri_dot = lambda part: jnp.dot(tril, part, preferred_element_type=F32)
    g_all = tri_dot(hi) + tri_dot(mid) + tri_dot(lo)

    for h in range(H_A):
        kc = slice(h * DK_A, (h + 1) * DK_A)
        vc = slice(h * DV_A, (h + 1) * DV_A)
        g_cum = g_all[:, kc]
        g_last = g_cum[blk - 1:blk, :]
        k = k_ref[:, kc]
        v = v_ref[:, vc]
        s_prev = s_scr[h]
        q_dec = q_ref[:, kc] * (DK_A ** -0.5) * jnp.exp(g_cum)
        k_dec = k * jnp.exp(-g_cum)
        att = lax.dot_general(q_dec, k_dec, (((1,), (1,)), ((), ())), preferred_element_type=F32)
        att = jnp.where(causal, att, 0.0)
        o = (jnp.dot(att, v, preferred_element_type=F32)
             + jnp.dot(q_dec, s_prev, preferred_element_type=F32))
        k_end = k * jnp.exp(g_last - g_cum)
        decay_col = jnp.transpose(jnp.broadcast_to(jnp.exp(g_last), (LANES, DK_A)))[:, 0:1]
        s_new = s_prev * decay_col + lax.dot_general(
            k_end, v, (((0,), (0,)), ((), ())), preferred_element_type=F32)
        s_scr[h] = s_new
        ms = jnp.mean(o * o, axis=-1, keepdims=True)
        r = r_ref[:, vc]
        o_ref[:, vc] = o * lax.rsqrt(ms + EPS) * on_ref[...] * (r * jax.nn.sigmoid(r))

    @pl.when(c == n_chunks - 1)
    def _():
        sout_ref[...] = s_scr[...]


def _gla_scan(qkv, gl, r, layer, w_g2, b_g2, onorm, s0, o_buf, *, row0, nb, t_len, name):
    blk = min(CHUNK, t_len)
    n_chunks = t_len // blk
    rb0 = row0 // blk
    has_s0 = s0 is not None
    rows = lambda b, c: rb0 + b * n_chunks + c
    in_specs = [
        pl.BlockSpec((blk, DK_TOT), lambda b, c: (rows(b, c), 0)),
        pl.BlockSpec((blk, DK_TOT), lambda b, c: (rows(b, c), 1)),
        pl.BlockSpec((blk, DV_TOT), lambda b, c: (rows(b, c), 2 * DK_TOT // DV_TOT)),
        pl.BlockSpec((blk, GATE_RANK), lambda b, c: (rows(b, c), 0)),
        pl.BlockSpec((None, GATE_RANK, DK_TOT), lambda b, c: (layer, 0, 0)),
        pl.BlockSpec((None, 1, DK_TOT), lambda b, c: (layer, 0, 0)),
        pl.BlockSpec((blk, DV_TOT), lambda b, c: (rows(b, c), 0)),
        pl.BlockSpec((None, 1, DV_A), lambda b, c: (layer, 0, 0)),
        pl.BlockSpec(memory_space=pl.ANY),
    ]
    args = [qkv, qkv, qkv, gl, w_g2, b_g2.reshape(N_A, 1, DK_TOT), r,
            onorm.reshape(N_A, 1, DV_A), o_buf]
    if has_s0:
        in_specs.append(pl.BlockSpec((None, None, H_A, DK_A, DV_A),
                                     lambda b, c: (layer, b, 0, 0, 0)))
        args.append(s0)
    return pl.pallas_call(
        functools.partial(_gla_kernel, blk=blk, n_chunks=n_chunks, has_s0=has_s0),
        out_shape=(jax.ShapeDtypeStruct(o_buf.shape, F32),
                   jax.ShapeDtypeStruct((nb, H_A, DK_A, DV_A), F32)),
        grid=(nb, n_chunks),
        in_specs=in_specs,
        out_specs=(pl.BlockSpec((blk, DV_TOT), lambda b, c: (rows(b, c), 0)),
                   pl.BlockSpec((None, H_A, DK_A, DV_A), lambda b, c: (b, 0, 0, 0))),
        scratch_shapes=[pltpu.VMEM((H_A, DK_A, DV_A), F32)],
        input_output_aliases={8: 0},
        compiler_params=_params(("arbitrary", "arbitrary")),
        name=name,
    )(*args)


def _cumsum_kernel(x_ref, o_ref, carry_ref, *, tb):
    @pl.when(pl.program_id(1) == 0)
    def _():
        carry_ref[...] = jnp.zeros_like(carry_ref)

    r_i = lax.broadcasted_iota(I32, (tb, tb), 0)
    c_i = lax.broadcasted_iota(I32, (tb, tb), 1)
    triu = (r_i <= c_i).astype(BF16)
    hi, mid, lo = _split3(x_ref[...])
    tri_dot = lambda part: jnp.dot(part, triu, preferred_element_type=F32)
    y = tri_dot(hi) + tri_dot(mid) + tri_dot(lo) + carry_ref[:, 0:1]
    o_ref[...] = y
    carry_ref[...] = jnp.broadcast_to(y[:, tb - 1:tb], carry_ref.shape)


def _cumsum_lanes(x, tb, name):
    nb, nh, t_len = x.shape
    spec = pl.BlockSpec((None, nh, tb), lambda b, j: (b, 0, j))
    return pl.pallas_call(
        functools.partial(_cumsum_kernel, tb=tb),
        out_shape=jax.ShapeDtypeStruct(x.shape, F32),
        grid=(nb, t_len // tb),
        in_specs=[spec],
        out_specs=spec,
        scratch_shapes=[pltpu.VMEM((nh, LANES), F32)],
        compiler_params=_params(("arbitrary", "arbitrary")),
        name=name,
    )(x)


ATT_TILE = 256
KV_TILE = 512


def _fox_prompt_kernel(q_ref, k_ref, v_ref, cq_ref, ck_ref, _obuf, o_ref, m_ref, l_ref, acc_ref,
                       *, n_k):
    qi = pl.program_id(1)
    ki = pl.program_id(2)
    last_k = (qi * ATT_TILE) // KV_TILE

    @pl.when(ki == 0)
    def _():
        m_ref[...] = jnp.full_like(m_ref, -jnp.inf)
        l_ref[...] = jnp.zeros_like(l_ref)
        acc_ref[...] = jnp.zeros_like(acc_ref)

    def accumulate(masked):
        if masked:
            qpos = qi * ATT_TILE + lax.broadcasted_iota(I32, (ATT_TILE, KV_TILE), 0)
            kpos = ki * KV_TILE + lax.broadcasted_iota(I32, (ATT_TILE, KV_TILE), 1)
            visible = kpos <= qpos
        for h in range(H_B):
            cols = slice(h * HD_B, (h + 1) * HD_B)
            s = lax.dot_general(q_ref[:, cols], k_ref[:, cols], (((1,), (1,)), ((), ())),
                                preferred_element_type=F32) * (HD_B ** -0.5)
            s = s + cq_ref[:, h:h + 1] - ck_ref[h:h + 1, :]
            if masked:
                s = jnp.where(visible, s, -jnp.inf)
            m_prev = m_ref[h]
            m_next = jnp.maximum(m_prev, jnp.max(s, axis=1, keepdims=True))
            alpha = jnp.exp(m_prev - m_next)
            p = jnp.exp(s - jnp.concatenate([m_next] * (KV_TILE // LANES), axis=1))
            l_ref[h] = alpha * l_ref[h] + jnp.sum(p, axis=1, keepdims=True)
            acc_ref[:, cols] = acc_ref[:, cols] * alpha + jnp.dot(
                p, v_ref[:, cols], preferred_element_type=F32)
            m_ref[h] = m_next

    pl.when(ki < last_k)(functools.partial(accumulate, False))
    pl.when(ki == last_k)(functools.partial(accumulate, True))

    @pl.when(ki == n_k - 1)
    def _():
        for h in range(H_B):
            cols = slice(h * HD_B, (h + 1) * HD_B)
            o_ref[:, cols] = acc_ref[:, cols] / l_ref[h]


def _fox_prompt(q, k, v, cq, ck_t, o_buf, geo, name):
    n_q = geo.tp // ATT_TILE
    n_k = geo.tp // KV_TILE
    used_k = lambda qi, ki: jnp.minimum(ki, (qi * ATT_TILE) // KV_TILE)
    kv_blk = lambda b, qi, ki: (b * n_k + used_k(qi, ki), 0)
    q_blk = lambda b, qi, ki: (b * n_q + qi, 0)
    return pl.pallas_call(
        functools.partial(_fox_prompt_kernel, n_k=n_k),
        out_shape=jax.ShapeDtypeStruct(o_buf.shape, F32),
        grid=(geo.bp, n_q, n_k),
        in_specs=[pl.BlockSpec((ATT_TILE, D_MODEL), q_blk),
                  pl.BlockSpec((KV_TILE, D_MODEL), kv_blk),
                  pl.BlockSpec((KV_TILE, D_MODEL), kv_blk),
                  pl.BlockSpec((ATT_TILE, H_B), q_blk),
                  pl.BlockSpec((None, H_B, KV_TILE),
                               lambda b, qi, ki: (b, 0, used_k(qi, ki))),
                  pl.BlockSpec(memory_space=pl.ANY)],
        out_specs=pl.BlockSpec((ATT_TILE, D_MODEL), q_blk),
        scratch_shapes=[pltpu.VMEM((H_B, ATT_TILE, LANES), F32),
                        pltpu.VMEM((H_B, ATT_TILE, LANES), F32),
                        pltpu.VMEM((ATT_TILE, D_MODEL), F32)],
        input_output_aliases={5: 0},
        compiler_params=_params(("arbitrary", "arbitrary", "arbitrary")),
        name=name,
    )(q, k, v, cq, ck_t, o_buf)


def _fox_sample_kernel(q_ref, kp_ref, vp_ref, kn_ref, vn_ref, cq_ref, ckp_ref, ckn_ref, _oprev,
                       o_ref):
    t_len = q_ref.shape[0]
    scale = HD_B ** -0.5
    nt_dims = (((1,), (1,)), ((), ()))
    r_i = lax.broadcasted_iota(I32, (t_len, t_len), 0)
    c_i = lax.broadcasted_iota(I32, (t_len, t_len), 1)
    causal = c_i <= r_i
    for h in range(H_B):
        cols = slice(h * HD_B, (h + 1) * HD_B)
        q = q_ref[:, cols]
        cq = cq_ref[:, h:h + 1]
        s_past = lax.dot_general(q, kp_ref[:, cols], nt_dims, preferred_element_type=F32) * scale
        s_past = s_past + cq - ckp_ref[h:h + 1, :]
        s_new = lax.dot_general(q, kn_ref[:, cols], nt_dims, preferred_element_type=F32) * scale
        s_new = jnp.where(causal, s_new + cq - ckn_ref[h:h + 1, :], -jnp.inf)
        m = jnp.maximum(jnp.max(s_past, axis=1, keepdims=True),
                        jnp.max(s_new, axis=1, keepdims=True))
        p_past = jnp.exp(s_past - m)
        p_new = jnp.exp(s_new - m)
        denom = jnp.sum(p_past, axis=1, keepdims=True) + jnp.sum(p_new, axis=1, keepdims=True)
        o = (jnp.dot(p_past, vp_ref[:, cols], preferred_element_type=F32)
             + jnp.dot(p_new, vn_ref[:, cols], preferred_element_type=F32))
        o_ref[:, cols] = o / denom


def _fox_sample(q, cache_k, cache_v, k_new, v_new, cq, ck_past, ck_new, o_prev, geo, name):
    rb0 = _n_prompt(geo) // geo.ts
    new_blk = pl.BlockSpec((geo.ts, D_MODEL), lambda b: (rb0 + b, 0))
    past_blk = pl.BlockSpec((None, geo.past, D_MODEL), lambda b: (b, 0, 0))
    return pl.pallas_call(
        _fox_sample_kernel,
        out_shape=jax.ShapeDtypeStruct(o_prev.shape, F32),
        grid=(geo.bs,),
        in_specs=[new_blk, past_blk, past_blk, new_blk, new_blk,
                  pl.BlockSpec((None, geo.ts, H_B), lambda b: (b, 0, 0)),
                  pl.BlockSpec((None, H_B, geo.past), lambda b: (b, 0, 0)),
                  pl.BlockSpec((None, H_B, geo.ts), lambda b: (b, 0, 0)),
                  pl.BlockSpec(memory_space=pl.ANY)],
        out_specs=new_blk,
        input_output_aliases={8: 0},
        compiler_params=_params(("arbitrary",)),
        name=name,
    )(q, cache_k, cache_v, k_new, v_new, cq, ck_past, ck_new, o_prev)


def _max_tiles(n_rows):
    return -(-(n_rows * TOP_K + N_EXPERTS * (ROW_TILE - 1)) // ROW_TILE)


def _route_plan(top_i, rank, counts, n_rows):
    counts = counts.reshape(N_EXPERTS).astype(I32)
    padded = ((counts + ROW_TILE - 1) // ROW_TILE) * ROW_TILE
    g_end = jnp.cumsum(padded)
    g_start = g_end - padded
    onehot = top_i[..., None] == jnp.arange(N_EXPERTS, dtype=I32)
    pos = jnp.sum(jnp.where(onehot, g_start, 0), axis=-1) + rank
    n_tiles = g_end[-1] // ROW_TILE
    max_tiles = _max_tiles(n_rows)
    tile_ids = jnp.arange(max_tiles, dtype=I32)
    tile_e = jnp.sum(((tile_ids * ROW_TILE)[:, None] >= g_end[None, :]).astype(I32), axis=1)
    last_e = jnp.sum((((n_tiles - 1) * ROW_TILE) >= g_end).astype(I32))
    tile_e = jnp.where(tile_ids < n_tiles, tile_e, last_e).astype(I32)
    first = (tile_ids == 0) | (tile_e != jnp.roll(tile_e, 1))
    end_tile = jnp.sum(jnp.where(tile_e[:, None] == jnp.arange(N_EXPERTS, dtype=I32), g_end, 0),
                       axis=1) // ROW_TILE
    next_e = jnp.where(end_tile < n_tiles, tile_e[jnp.minimum(end_tile, max_tiles - 1)], -1)
    tiles = (tile_e, n_tiles.reshape(1).astype(I32), first.astype(I32), next_e.astype(I32))
    return (pos.reshape(-1).astype(I32), tiles,
            (g_start + counts).astype(I32), (padded - counts).astype(I32))


DRAIN_UNROLL = 16
DMA_THREADS = 2


def _token_slab(ref, token):
    row0 = token * SLAB_ROWS
    if not isinstance(row0, int):
        row0 = pl.multiple_of(row0, SLAB_ROWS)
    return ref.at[:, pl.ds(row0, SLAB_ROWS), :]


def _token_copy(src, src_token, dst, dst_token, sem):
    return pltpu.make_async_copy(_token_slab(src, src_token), _token_slab(dst, dst_token), sem)


def _dispatch_kernel(pos_ref, pad0_ref, padn_ref, h_ref, xs_ref, zero_ref, sem):
    i = pl.program_id(0)

    @pl.when(i == 0)
    def _():
        zero_ref[...] = jnp.zeros_like(zero_ref)

        def per_expert(e, carry):
            start = pad0_ref[e]
            count = padn_ref[e]

            def issue(j, c):
                _token_copy(zero_ref, 0, xs_ref, start + j, sem).start()
                return c

            def drain(j, c):
                _token_copy(zero_ref, 0, xs_ref, start + j, sem).wait()
                return c

            lax.fori_loop(0, count, issue, 0)
            lax.fori_loop(0, count, drain, 0)
            return carry

        lax.fori_loop(0, N_EXPERTS, per_expert, 0)

    def issue_row(r, c):
        for k in range(TOP_K):
            p = pos_ref[(i * ROW_TILE + r) * TOP_K + k]
            _token_copy(h_ref, r, xs_ref, p, sem).start(priority=k % DMA_THREADS)
        return c

    def drain_rows(g, c):
        for _ in range(DRAIN_UNROLL * TOP_K):
            _token_copy(h_ref, 0, xs_ref, 0, sem).wait()
        return c

    lax.fori_loop(0, ROW_TILE, issue_row, 0)
    lax.fori_loop(0, ROW_TILE // DRAIN_UNROLL, drain_rows, 0)


def _dispatch(h_tm, pos, pad0, padn, layer):
    n = h_tm.shape[1] // SLAB_ROWS
    rows_sorted = _max_tiles(n) * ROW_TILE
    grid_spec = pltpu.PrefetchScalarGridSpec(
        num_scalar_prefetch=3,
        grid=(n // ROW_TILE,),
        in_specs=[pl.BlockSpec(_tm_shape(ROW_TILE), lambda i, *_: (0, i, 0))],
        out_specs=pl.BlockSpec(memory_space=pl.ANY),
        scratch_shapes=[pltpu.VMEM(_tm_shape(1), F32), pltpu.SemaphoreType.DMA],
    )
    return pl.pallas_call(
        _dispatch_kernel,
        out_shape=jax.ShapeDtypeStruct(_tm_shape(rows_sorted), F32),
        grid_spec=grid_spec,
        compiler_params=_params(("arbitrary",)),
        name=f"moe_dispatch_{layer}",
    )(pos, pad0, padn, h_tm)


FF_TILE = 1024


W_CHUNK = 256


def _expert_weights_step(fetch, te_ref, first_ref, next_ref, stage_ref, cache_ref, n_pass):
    n = pl.program_id(0)
    r = pl.program_id(1)

    @pl.when(first_ref[r] == 1)
    def _():
        @pl.when((n == 0) & (r == 0))
        def _():
            for copy in fetch(te_ref[0], 0):
                copy.start()

        for copy in fetch(te_ref[r], n):
            copy.wait()
        for part in range(stage_ref.shape[0]):
            for c in range(stage_ref.shape[1] // W_CHUNK):
                rows = pl.ds(c * W_CHUNK, W_CHUNK)
                cache_ref[part, rows, :] = stage_ref[part, rows, :].astype(BF16)
        nxt = next_ref[r]

        @pl.when(nxt >= 0)
        def _():
            for copy in fetch(nxt, n):
                copy.start()

        @pl.when((nxt < 0) & (n + 1 < n_pass))
        def _():
            for copy in fetch(te_ref[0], n + 1):
                copy.start()


def _moe_up_kernel(te_ref, nt_ref, first_ref, next_ref, xs_ref, w_ref, bg_ref, bu_ref, a_ref,
                   stage_ref, cache_ref, sems, *, layer):
    def fetch(e, n):
        col0 = n * FF_TILE
        if not isinstance(col0, int):
            col0 = pl.multiple_of(col0, FF_TILE)
        return [pltpu.make_async_copy(
            w_ref.at[layer, e, :, pl.ds(col0 + part * D_FF, FF_TILE)], stage_ref.at[part],
            sems.at[part]) for part in range(2)]

    @pl.when(pl.program_id(1) < nt_ref[0])
    def _():
        _expert_weights_step(fetch, te_ref, first_ref, next_ref, stage_ref, cache_ref,
                             D_FF // FF_TILE)
        x = _load_token_major(xs_ref, ROW_TILE).astype(BF16)
        g = jnp.dot(x, cache_ref[0], preferred_element_type=F32) + bg_ref[...]
        u = jnp.dot(x, cache_ref[1], preferred_element_type=F32) + bu_ref[...]
        g = jnp.minimum(g, SWIGLU_LIMIT)
        u = jnp.clip(u, -SWIGLU_LIMIT, SWIGLU_LIMIT)
        a_ref[...] = (g * jax.nn.sigmoid(SWIGLU_ALPHA * g) * (u + 1.0)).astype(BF16)


def _moe_down_kernel(te_ref, nt_ref, first_ref, next_ref, a_ref, w_ref, b_ref, y_ref,
                     stage_ref, cache_ref, sems, *, layer):
    def fetch(e, n):
        return [pltpu.make_async_copy(w_ref.at[layer, e], stage_ref.at[0], sems.at[0])]

    @pl.when(pl.program_id(1) < nt_ref[0])
    def _():
        _expert_weights_step(fetch, te_ref, first_ref, next_ref, stage_ref, cache_ref, 1)
        y = jnp.dot(a_ref[...], cache_ref[0], preferred_element_type=F32) + b_ref[...]
        _store_token_major(y_ref, y)


def _moe_experts(xs_tm, tiles, w_in, b_in, w_out, b_out, layer):
    rows_sorted = xs_tm.shape[1] // SLAB_ROWS
    max_tiles = rows_sorted // ROW_TILE
    u_blk0 = D_FF // FF_TILE
    used = lambda r, nt: jnp.minimum(r, nt[0] - 1)
    tm_spec = pl.BlockSpec(_tm_shape(ROW_TILE), lambda n, r, te, nt, fi, nx: (0, used(r, nt), 0))
    up_spec = pltpu.PrefetchScalarGridSpec(
        num_scalar_prefetch=4,
        grid=(D_FF // FF_TILE, max_tiles),
        in_specs=[
            tm_spec,
            pl.BlockSpec(memory_space=pl.ANY),
            pl.BlockSpec((None, None, 1, FF_TILE),
                         lambda n, r, te, nt, fi, nx: (layer, te[r], 0, n)),
            pl.BlockSpec((None, None, 1, FF_TILE),
                         lambda n, r, te, nt, fi, nx: (layer, te[r], 0, u_blk0 + n)),
        ],
        out_specs=pl.BlockSpec((ROW_TILE, FF_TILE),
                               lambda n, r, te, nt, fi, nx: (used(r, nt), n)),
        scratch_shapes=[pltpu.VMEM((2, D_MODEL, FF_TILE), F32),
                        pltpu.VMEM((2, D_MODEL, FF_TILE), BF16),
                        pltpu.SemaphoreType.DMA((2,))],
    )
    b_in4 = b_in.reshape(DEPTH, N_EXPERTS, 1, 2 * D_FF)
    act = pl.pallas_call(
        functools.partial(_moe_up_kernel, layer=layer),
        out_shape=jax.ShapeDtypeStruct((rows_sorted, D_FF), BF16),
        grid_spec=up_spec,
        compiler_params=_params(("arbitrary", "arbitrary")),
        name=f"moe_up_{layer}",
    )(*tiles, xs_tm, w_in, b_in4, b_in4)
    down_spec = pltpu.PrefetchScalarGridSpec(
        num_scalar_prefetch=4,
        grid=(1, max_tiles),
        in_specs=[
            pl.BlockSpec((ROW_TILE, D_FF), lambda n, r, te, nt, fi, nx: (used(r, nt), 0)),
            pl.BlockSpec(memory_space=pl.ANY),
            pl.BlockSpec((None, None, 1, D_MODEL),
                         lambda n, r, te, nt, fi, nx: (layer, te[r], 0, 0)),
        ],
        out_specs=tm_spec,
        scratch_shapes=[pltpu.VMEM((1, D_FF, D_MODEL), F32),
                        pltpu.VMEM((1, D_FF, D_MODEL), BF16),
                        pltpu.SemaphoreType.DMA((1,))],
    )
    return pl.pallas_call(
        functools.partial(_moe_down_kernel, layer=layer),
        out_shape=jax.ShapeDtypeStruct(_tm_shape(rows_sorted), F32),
        grid_spec=down_spec,
        compiler_params=_params(("arbitrary", "arbitrary")),
        name=f"moe_down_{layer}",
    )(*tiles, act, w_out, b_out.reshape(DEPTH, N_EXPERTS, 1, D_MODEL))


def _combine_kernel(pos_ref, ys_ref, tw_ref, x_ref, gp_ref, gs_ref, o_ref, buf_ref, wexp_ref, sem,
                    *, n_prompt_tiles):
    i = pl.program_id(0)

    def issue_row(r, c):
        for k in range(TOP_K):
            p = pos_ref[(i * ROW_TILE + r) * TOP_K + k]
            _token_copy(ys_ref, p, buf_ref.at[k], r, sem).start(priority=k % DMA_THREADS)
        return c

    def drain_rows(g, c):
        for _ in range(DRAIN_UNROLL * TOP_K):
            _token_copy(ys_ref, 0, buf_ref.at[0], 0, sem).wait()
        return c

    lax.fori_loop(0, ROW_TILE, issue_row, 0)
    lax.fori_loop(0, ROW_TILE // DRAIN_UNROLL, drain_rows, 0)
    tw = tw_ref[...]
    for j in range(SLAB_ROWS):
        wexp_ref[pl.ds(j, ROW_TILE, stride=SLAB_ROWS), :] = tw
    for c in range(SLAB_ROWS):
        rows = pl.ds(c * ROW_TILE, ROW_TILE)
        w_cols = [wexp_ref[rows, k:k + 1] for k in range(TOP_K)]
        for p in range(SLAB_PARTS):
            y_tm = w_cols[0] * buf_ref[0, p, rows, :]
            for k in range(1, TOP_K):
                y_tm = y_tm + w_cols[k] * buf_ref[k, p, rows, :]
            buf_ref[0, p, rows, :] = y_tm
    y = _load_token_major(buf_ref.at[0], ROW_TILE)
    gate = jnp.where(i < n_prompt_tiles, gp_ref[...], gs_ref[...])
    o_ref[...] = x_ref[...] + gate * y


def _combine(ys, pos, top_w, x, gate_tabs, gate_comp, geo, layer):
    n = x.shape[0]
    gp, gs = _mod_specs(gate_comp, geo, lambda i, *_: i, lambda i, *_: 0, D_MODEL)
    x_spec = pl.BlockSpec((ROW_TILE, D_MODEL), lambda i, *_: (i, 0))
    grid_spec = pltpu.PrefetchScalarGridSpec(
        num_scalar_prefetch=1,
        grid=(n // ROW_TILE,),
        in_specs=[pl.BlockSpec(memory_space=pl.ANY),
                  pl.BlockSpec((ROW_TILE, LANES), lambda i, *_: (i, 0)),
                  x_spec, gp, gs],
        out_specs=x_spec,
        scratch_shapes=[pltpu.VMEM((TOP_K,) + _tm_shape(ROW_TILE), F32),
                        pltpu.VMEM((ROW_TILE * SLAB_ROWS, LANES), F32),
                        pltpu.SemaphoreType.DMA],
    )
    return pl.pallas_call(
        functools.partial(_combine_kernel, n_prompt_tiles=_n_prompt(geo) // ROW_TILE),
        out_shape=jax.ShapeDtypeStruct((n, D_MODEL), F32),
        grid_spec=grid_spec,
        compiler_params=_params(("arbitrary",)),
        name=f"moe_combine_{layer}",
    )(pos, ys, top_w, x, gate_tabs[0], gate_tabs[1])


def _trunk(x, cs, state_gla, cache_k, cache_v, cache_logf, geo,
           w_mod, b_mod, norm_mix, norm_ffn,
           w_gla_in, w_gla_g2, b_gla_g2, gla_onorm, w_gla_out,
           w_kv_mod, b_kv_mod, norm_kv, w_kv, b_f, k_norm,
           w_fox_q, q_norm, w_fox_out,
           w_router, b_router, w_moe_in, b_moe_in, w_moe_out, b_moe_out):
    n = _n_rows(geo)
    n_p = _n_prompt(geo)
    b_mod3 = b_mod.reshape(DEPTH, 1, 6 * D_MODEL)
    gla_p, gla_s = [], []
    kv = None
    for layer in range(DEPTH):
        if layer == N_A:
            kv_mod = _mm(cs, w_kv_mod, ncols=2 * D_MODEL, tn=1024, tm=MOD_ROWS, epi="bias",
                         bias=b_kv_mod.reshape(1, 1, 2 * D_MODEL), name="kv_mod")
            kv_tabs = _mod_tables(kv_mod, 2, geo)
            hk = _norm_mod(x, norm_kv, kv_tabs, 0, 1, geo, "norm_kv")
            k_new = _mm(hk, w_kv, ncols=D_MODEL, tn=1024, epi="headnorm", gain=k_norm, name="kv_k")
            v_new = _mm(hk, w_kv, ncols=D_MODEL, tn=1024, col0=D_MODEL, name="kv_v")
            logf = _mm(hk, w_kv[:, 2 * D_MODEL:], ncols=H_B, tn=H_B, epi="logsig",
                       bias=b_f.reshape(1, 1, H_B), name="kv_f")
            logf_p = logf[:n_p].reshape(geo.bp, geo.tp, H_B)
            logf_s = logf[n_p:].reshape(geo.bs, geo.ts, H_B)
            c_p_t = _cumsum_lanes(logf_p.transpose(0, 2, 1), 512, "cumsum_p")
            cq_p = c_p_t.transpose(0, 2, 1).reshape(n_p, H_B)
            all_s = jnp.concatenate([cache_logf.astype(F32), logf_s], axis=1).transpose(0, 2, 1)
            c_s_t = _cumsum_lanes(all_s, geo.past + geo.ts, "cumsum_s")
            ck_past = c_s_t[:, :, :geo.past]
            ck_new = c_s_t[:, :, geo.past:]
            cq_s = ck_new.transpose(0, 2, 1)
            cache_k2 = cache_k.reshape(geo.bs, geo.past, D_MODEL)
            cache_v2 = cache_v.reshape(geo.bs, geo.past, D_MODEL)
            kv = (k_new, v_new, logf_p, logf_s)

        mod = _mm(cs, w_mod, wl=layer, ncols=6 * D_MODEL, tn=1024, tm=MOD_ROWS, epi="bias",
                  bias=b_mod3, bl=layer, name=f"mod_{layer}")
        tabs = _mod_tables(mod, 6, geo)
        h = _norm_mod(x, norm_mix[layer], tabs, 0, 1, geo, f"norm_mix_{layer}")
        if layer < N_A:
            qkv = _mm(h, w_gla_in, wl=layer, ncols=2 * DK_TOT + DV_TOT, tn=1024, name=f"gla_qkv_{layer}")
            gcol = 2 * DK_TOT + DV_TOT
            gl = _mm(h, w_gla_in[layer, :, gcol:gcol + GATE_RANK], ncols=GATE_RANK, tn=GATE_RANK,
                     name=f"gla_gl_{layer}")
            r = _mm(h, w_gla_in[layer, :, gcol + GATE_RANK:], ncols=DV_TOT, tn=1024,
                    name=f"gla_r_{layer}")
            o, s_p = _gla_scan(qkv, gl, r, layer, w_gla_g2, b_gla_g2, gla_onorm, None, h,
                               row0=0, nb=geo.bp, t_len=geo.tp, name=f"gla_scan_p_{layer}")
            o, s_s = _gla_scan(qkv, gl, r, layer, w_gla_g2, b_gla_g2, gla_onorm, state_gla, o,
                               row0=n_p, nb=geo.bs, t_len=geo.ts, name=f"gla_scan_s_{layer}")
            gla_p.append(s_p)
            gla_s.append(s_s)
            x = _mm(o, w_gla_out, wl=layer, ncols=D_MODEL, tn=1024, epi="resgate", xres=x,
                    gate_tabs=tabs, gate_comp=2, geo=geo, name=f"gla_out_{layer}")
        else:
            j = layer - N_A
            q = _mm(h, w_fox_q, wl=j, ncols=D_MODEL, tn=1024, epi="headnorm", gain=q_norm[j],
                    name=f"fox_q_{j}")
            o = _fox_prompt(q, kv[0], kv[1], cq_p, c_p_t, h, geo, f"fox_attn_p_{j}")
            o = _fox_sample(q, cache_k2, cache_v2, kv[0], kv[1], cq_s, ck_past, ck_new, o, geo,
                            f"fox_attn_s_{j}")
            x = _mm(o, w_fox_out, wl=j, ncols=D_MODEL, tn=1024, epi="resgate", xres=x,
                    gate_tabs=tabs, gate_comp=2, geo=geo, name=f"fox_out_{j}")

        h2, top_i, top_w, rank, counts = _norm_router(x, norm_ffn[layer], tabs, 3, 4,
                                                      w_router, b_router, layer, geo)
        pos, tiles, pad0, padn = _route_plan(top_i[:, :TOP_K], rank[:, :TOP_K], counts, n)
        xs = _dispatch(h2, pos, pad0, padn, layer)
        ys = _moe_experts(xs, tiles, w_moe_in, b_moe_in, w_moe_out, b_moe_out, layer)
        x = _combine(ys, pos, top_w, x, tabs, 5, geo, layer)
    return x, gla_p, gla_s, kv


def kernel(x_prompt, x_sample, c_prompt, c_sample, state_gla, cache_k, cache_v, cache_logf, w_mod, b_mod, norm_mix, norm_ffn, w_gla_in, w_gla_g2, b_gla_g2, gla_onorm, w_gla_out, w_kv_mod, b_kv_mod, norm_kv, w_kv, b_f, k_norm, w_fox_q, q_norm, w_fox_out, w_router, b_router, w_moe_in, b_moe_in, w_moe_out, b_moe_out):
    bp, tp, _ = x_prompt.shape
    bs, ts, _ = x_sample.shape
    geo = Geo(bp, tp, bs, ts, cache_k.shape[1])
    assert bs * ts == ROW_TILE and tp % ROW_TILE == 0 and tp % 512 == 0 and bp + bs <= MOD_ROWS
    n_p = bp * tp
    x = jnp.concatenate([x_prompt.reshape(n_p, D_MODEL), x_sample.reshape(bs * ts, D_MODEL)], axis=0)
    c = jnp.concatenate([c_prompt, c_sample], axis=0).astype(F32)
    cs = jnp.pad(c * jax.nn.sigmoid(c), ((0, MOD_ROWS - bp - bs), (0, 0)))
    x, gla_p, gla_s, (k_new, v_new, logf_p, logf_s) = _trunk(
        x, cs, state_gla, cache_k, cache_v, cache_logf, geo,
        w_mod, b_mod, norm_mix, norm_ffn,
        w_gla_in, w_gla_g2, b_gla_g2, gla_onorm, w_gla_out,
        w_kv_mod, b_kv_mod, norm_kv, w_kv, b_f, k_norm,
        w_fox_q, q_norm, w_fox_out,
        w_router, b_router, w_moe_in, b_moe_in, w_moe_out, b_moe_out)
    return (x[:n_p].reshape(bp, tp, D_MODEL),
            x[n_p:].reshape(bs, ts, D_MODEL),
            jnp.stack(gla_p), jnp.stack(gla_s),
            k_new[:n_p].reshape(bp, tp, H_B, HD_B),
            v_new[:n_p].reshape(bp, tp, H_B, HD_B),
            logf_p,
            k_new[n_p:].reshape(bs, ts, H_B, HD_B),
            v_new[n_p:].reshape(bs, ts, H_B, HD_B),
            logf_s)
```

```python
import collections
import functools

import jax
import jax.numpy as jnp
from jax import lax
from jax.experimental import pallas as pl
from jax.experimental.pallas import tpu as pltpu

F32 = jnp.float32
BF16 = jnp.bfloat16
I32 = jnp.int32

D_MODEL = 2048
DEPTH = 4
N_A = 2
H_A = 4
DK_A = 256
DV_A = 512
DK_TOT = H_A * DK_A
DV_TOT = H_A * DV_A
GATE_RANK = 16
GATE_NORMALIZER = 16.0
H_B = 16
HD_B = 128
N_EXPERTS = 32
TOP_K = 4
D_FF = D_MODEL
SWIGLU_LIMIT = 7.0
SWIGLU_ALPHA = 1.702
EPS = 1e-6
CHUNK = 64

ROW_TILE = 256
LANES = 128
SUBLANES = 8
SLAB_ROWS = SUBLANES
SLAB_PARTS = D_MODEL // (SLAB_ROWS * LANES)


def _tm_shape(rows):
    return (SLAB_PARTS, rows * SLAB_ROWS, LANES)
MOD_ROWS = 24
VMEM_LIMIT_BYTES = 48 * 1024 * 1024

Geo = collections.namedtuple("Geo", "bp tp bs ts past")


def _n_prompt(geo):
    return geo.bp * geo.tp


def _n_rows(geo):
    return geo.bp * geo.tp + geo.bs * geo.ts


def _params(sem):
    return pltpu.CompilerParams(dimension_semantics=sem, vmem_limit_bytes=VMEM_LIMIT_BYTES)


def _log_sigmoid(x):
    return jnp.minimum(x, 0.0) - jnp.log1p(jnp.exp(-jnp.abs(x)))


def _store_token_major(ref, val):
    rows = val.shape[0]
    for j in range(SLAB_ROWS):
        for p in range(SLAB_PARTS):
            g = j * SLAB_PARTS + p
            ref[p, pl.ds(j, rows, stride=SLAB_ROWS), :] = val[:, g * LANES:(g + 1) * LANES]


def _load_token_major(ref, rows):
    return jnp.concatenate(
        [ref[p, pl.ds(j, rows, stride=SLAB_ROWS), :]
         for j in range(SLAB_ROWS) for p in range(SLAB_PARTS)], axis=1)


U32 = jnp.uint32
HIGH_HALF = 0xFFFF0000


def _store_packed_token_major(ref, val):
    rows = val.shape[0]
    half = D_MODEL // 2
    lo = lax.bitcast_convert_type(val[:, :half].astype(BF16).astype(F32), U32)
    hi = lax.bitcast_convert_type(val[:, half:].astype(BF16).astype(F32), U32)
    packed = (lo >> 16) | (hi & U32(HIGH_HALF))
    for j in range(SLAB_ROWS):
        ref[pl.ds(j, rows, stride=SLAB_ROWS), :] = packed[:, j * LANES:(j + 1) * LANES]


def _load_packed_token_major(ref, rows):
    lo, hi = [], []
    for j in range(SLAB_ROWS):
        word = ref[pl.ds(j, rows, stride=SLAB_ROWS), :]
        lo.append(lax.bitcast_convert_type(word << 16, F32).astype(BF16))
        hi.append(lax.bitcast_convert_type(word & U32(HIGH_HALF), F32).astype(BF16))
    return jnp.concatenate(lo + hi, axis=1)


def _split3(x):
    hi = x.astype(BF16)
    r1 = x - hi.astype(F32)
    mid = r1.astype(BF16)
    lo = (r1 - mid.astype(F32)).astype(BF16)
    return hi, mid, lo


def _mod_tables(mod, ncomp, geo):
    m = mod.reshape(mod.shape[0], ncomp, D_MODEL)
    tab_p = m[:geo.bp].transpose(1, 0, 2).reshape(ncomp, geo.bp, 1, D_MODEL)
    tab_s = jnp.repeat(m[geo.bp:geo.bp + geo.bs], geo.ts, axis=0).transpose(1, 0, 2)
    return tab_p, tab_s


def _mod_specs(comp, geo, tile_of, col_of, width):
    tiles_per_batch = geo.tp // ROW_TILE

    def idx_p(*ids):
        return (comp, jnp.minimum(tile_of(*ids) // tiles_per_batch, geo.bp - 1), 0, col_of(*ids))

    def idx_s(*ids):
        return (comp, 0, col_of(*ids))

    return (pl.BlockSpec((None, None, 1, width), idx_p),
            pl.BlockSpec((None, ROW_TILE, width), idx_s))


def _norm_mod_tile(x, g, sh_p, sc_p, sh_s, sc_s, is_prompt):
    ms = jnp.mean(x * x, axis=-1, keepdims=True)
    xn = x * lax.rsqrt(ms + EPS) * g
    shift = jnp.where(is_prompt, sh_p, sh_s)
    scale = jnp.where(is_prompt, sc_p, sc_s)
    return xn * (1.0 + scale) + shift


def _norm_mod_kernel(x_ref, g_ref, shp_ref, scp_ref, shs_ref, scs_ref, h_ref, *, n_prompt_tiles):
    is_prompt = pl.program_id(0) < n_prompt_tiles
    h_ref[...] = _norm_mod_tile(x_ref[...], g_ref[...], shp_ref[...], scp_ref[...],
                                shs_ref[...], scs_ref[...], is_prompt)


def _norm_mod(x, g, tabs, comp_shift, comp_scale, geo, name):
    n = x.shape[0]
    tab_p, tab_s = tabs
    row = lambda i: i
    col = lambda i: 0
    shp, shs = _mod_specs(comp_shift, geo, row, col, D_MODEL)
    scp, scs = _mod_specs(comp_scale, geo, row, col, D_MODEL)
    x_spec = pl.BlockSpec((ROW_TILE, D_MODEL), lambda i: (i, 0))
    return pl.pallas_call(
        functools.partial(_norm_mod_kernel, n_prompt_tiles=_n_prompt(geo) // ROW_TILE),
        out_shape=jax.ShapeDtypeStruct((n, D_MODEL), F32),
        grid=(n // ROW_TILE,),
        in_specs=[x_spec, pl.BlockSpec((1, D_MODEL), lambda i: (0, 0)), shp, scp, shs, scs],
        out_specs=x_spec,
        compiler_params=_params(("arbitrary",)),
        name=name,
    )(x, g.reshape(1, D_MODEL), tab_p, tab_p, tab_s, tab_s)


def _norm_router_kernel(x_ref, g_ref, shp_ref, scp_ref, shs_ref, scs_ref, wr_ref, br_ref,
                        h_ref, ti_ref, tw_ref, rk_ref, cnt_ref, run_ref, *, n_prompt_tiles):
    i = pl.program_id(0)

    @pl.when(i == 0)
    def _():
        run_ref[...] = jnp.zeros_like(run_ref)

    h = _norm_mod_tile(x_ref[...], g_ref[...], shp_ref[...], scp_ref[...],
                       shs_ref[...], scs_ref[...], i < n_prompt_tiles)
    _store_packed_token_major(h_ref, h)
    logits = jnp.dot(h, wr_ref[...], preferred_element_type=F32) + br_ref[...]
    lane = lax.broadcasted_iota(I32, logits.shape, 1)
    vals, idxs = [], []
    cur = logits
    for _ in range(TOP_K):
        m = jnp.max(cur, axis=-1, keepdims=True)
        idx = jnp.min(jnp.where(cur == m, lane, N_EXPERTS), axis=-1, keepdims=True)
        vals.append(m)
        idxs.append(idx)
        cur = jnp.where(lane == idx, -jnp.inf, cur)
    exps = [jnp.exp(v - vals[0]) for v in vals]
    denom = exps[0] + exps[1] + exps[2] + exps[3]

    r_i = lax.broadcasted_iota(I32, (ROW_TILE, ROW_TILE), 0)
    c_i = lax.broadcasted_iota(I32, (ROW_TILE, ROW_TILE), 1)
    tril_excl = (c_i < r_i).astype(F32)
    base = run_ref[...]
    ranks = []
    for k in range(TOP_K):
        onehot = (lane == idxs[k]).astype(F32)
        prefix = jnp.dot(tril_excl, onehot, preferred_element_type=F32)
        ranks.append(jnp.sum(onehot * (prefix + base), axis=-1, keepdims=True))
        base = base + jnp.sum(onehot, axis=0, keepdims=True)
    run_ref[...] = base
    cnt_ref[...] = base

    lane_out = lax.broadcasted_iota(I32, (ROW_TILE, LANES), 1)
    ti = jnp.zeros((ROW_TILE, LANES), I32)
    tw = jnp.zeros((ROW_TILE, LANES), F32)
    rk = jnp.zeros((ROW_TILE, LANES), I32)
    for k in range(TOP_K):
        sel = lane_out == k
        ti = jnp.where(sel, idxs[k], ti)
        tw = jnp.where(sel, exps[k] / denom, tw)
        rk = jnp.where(sel, ranks[k].astype(I32), rk)
    ti_ref[...] = ti
    tw_ref[...] = tw
    rk_ref[...] = rk


def _norm_router(x, g, tabs, comp_shift, comp_scale, w_router, b_router, layer, geo):
    n = x.shape[0]
    tab_p, tab_s = tabs
    row = lambda i: i
    col = lambda i: 0
    shp, shs = _mod_specs(comp_shift, geo, row, col, D_MODEL)
    scp, scs = _mod_specs(comp_scale, geo, row, col, D_MODEL)
    x_spec = pl.BlockSpec((ROW_TILE, D_MODEL), lambda i: (i, 0))
    lane_spec = pl.BlockSpec((ROW_TILE, LANES), lambda i: (i, 0))
    tm_spec = pl.BlockSpec((ROW_TILE * SLAB_ROWS, LANES), lambda i: (i, 0))
    return pl.pallas_call(
        functools.partial(_norm_router_kernel, n_prompt_tiles=_n_prompt(geo) // ROW_TILE),
        out_shape=(jax.ShapeDtypeStruct((n * SLAB_ROWS, LANES), U32),
                   jax.ShapeDtypeStruct((n, LANES), I32),
                   jax.ShapeDtypeStruct((n, LANES), F32),
                   jax.ShapeDtypeStruct((n, LANES), I32),
                   jax.ShapeDtypeStruct((1, N_EXPERTS), F32)),
        grid=(n // ROW_TILE,),
        in_specs=[x_spec, pl.BlockSpec((1, D_MODEL), lambda i: (0, 0)), shp, scp, shs, scs,
                  pl.BlockSpec((None, D_MODEL, N_EXPERTS), lambda i: (layer, 0, 0)),
                  pl.BlockSpec((None, 1, N_EXPERTS), lambda i: (layer, 0, 0))],
        out_specs=(tm_spec, lane_spec, lane_spec, lane_spec,
                   pl.BlockSpec((1, N_EXPERTS), lambda i: (0, 0))),
        scratch_shapes=[pltpu.VMEM((1, N_EXPERTS), F32)],
        compiler_params=_params(("arbitrary",)),
        name=f"norm_router_{layer}",
    )(x, g.reshape(1, D_MODEL), tab_p, tab_p, tab_s, tab_s, w_router,
      b_router.reshape(DEPTH, 1, N_EXPERTS))


def _mm_kernel(*refs, epi, n_prompt_tiles, tiles_per_batch):
    x_ref, w_ref = refs[0], refs[1]
    o_ref = refs[-1]
    acc = jnp.dot(x_ref[...], w_ref[...], preferred_element_type=F32)
    if epi == "bias":
        o_ref[...] = acc + refs[2][...]
    elif epi == "logsig":
        o_ref[...] = _log_sigmoid(acc + refs[2][...])
    elif epi == "headnorm":
        g = refs[2][...]
        for j in range(acc.shape[1] // HD_B):
            blk = acc[:, j * HD_B:(j + 1) * HD_B]
            ms = jnp.mean(blk * blk, axis=-1, keepdims=True)
            o_ref[:, j * HD_B:(j + 1) * HD_B] = blk * lax.rsqrt(ms + EPS) * g
    elif epi == "resgate":
        xres_ref, gp_ref, gs_ref = refs[2], refs[3], refs[4]
        tiles_per_step = acc.shape[0] // ROW_TILE
        for j in range(tiles_per_step):
            tile = pl.program_id(1) * tiles_per_step + j
            batch = jnp.minimum(tile // tiles_per_batch, gp_ref.shape[0] - 1)
            gate = jnp.where(tile < n_prompt_tiles, gp_ref[batch], gs_ref[...])
            rows = slice(j * ROW_TILE, (j + 1) * ROW_TILE)
            o_ref[rows, :] = xres_ref[rows, :] + gate * acc[rows, :]
    else:
        o_ref[...] = acc


def _dense_row_tile(m_rows):
    return max(t for t in (ROW_TILE, 2 * ROW_TILE, 3 * ROW_TILE) if m_rows % t == 0)


def _mm(x, w, *, ncols, tn, name, wl=None, col0=0, tm=None, epi="none", bias=None, bl=0,
        gain=None, xres=None, gate_tabs=None, gate_comp=0, geo=None):
    m_rows, k_dim = x.shape
    if tm is None:
        tm = _dense_row_tile(m_rows)
    cb0 = col0 // tn
    grid = (ncols // tn, m_rows // tm)
    x_spec = pl.BlockSpec((tm, k_dim), lambda n, m: (m, 0))
    if wl is None:
        w_spec = pl.BlockSpec((k_dim, tn), lambda n, m: (0, cb0 + n))
    else:
        w_spec = pl.BlockSpec((None, k_dim, tn), lambda n, m: (wl, 0, cb0 + n))
    o_spec = pl.BlockSpec((tm, tn), lambda n, m: (m, n))
    args, specs = [x, w], [x_spec, w_spec]
    n_prompt_tiles = tiles_per_batch = 0
    if epi in ("bias", "logsig"):
        args.append(bias)
        specs.append(pl.BlockSpec((None, 1, tn), lambda n, m: (bl, 0, cb0 + n)))
    elif epi == "headnorm":
        args.append(gain.reshape(1, HD_B))
        specs.append(pl.BlockSpec((1, HD_B), lambda n, m: (0, 0)))
    elif epi == "resgate":
        n_prompt_tiles = _n_prompt(geo) // ROW_TILE
        tiles_per_batch = geo.tp // ROW_TILE
        args += [xres, gate_tabs[0], gate_tabs[1]]
        specs += [o_spec,
                  pl.BlockSpec((None, geo.bp, 1, tn), lambda n, m: (gate_comp, 0, 0, n)),
                  pl.BlockSpec((None, ROW_TILE, tn), lambda n, m: (gate_comp, 0, n))]
    return pl.pallas_call(
        functools.partial(_mm_kernel, epi=epi, n_prompt_tiles=n_prompt_tiles,
                          tiles_per_batch=tiles_per_batch),
        out_shape=jax.ShapeDtypeStruct((m_rows, ncols), F32),
        grid=grid,
        in_specs=specs,
        out_specs=o_spec,
        compiler_params=_params(("arbitrary", "arbitrary")),
        name=name,
    )(*args)


def _gla_kernel(*refs, blk, n_chunks, has_s0):
    if has_s0:
        (q_ref, k_ref, v_ref, gl_ref, wg2_ref, bg2_ref, r_ref, on_ref, _obuf, s0_ref,
         o_ref, sout_ref, s_scr) = refs
    else:
        (q_ref, k_ref, v_ref, gl_ref, wg2_ref, bg2_ref, r_ref, on_ref, _obuf,
         o_ref, sout_ref, s_scr) = refs
    c = pl.program_id(1)

    @pl.when(c == 0)
    def _():
        if has_s0:
            s_scr[...] = s0_ref[...]
        else:
            s_scr[...] = jnp.zeros_like(s_scr)

    log_a = _log_sigmoid(jnp.dot(gl_ref[...], wg2_ref[...], preferred_element_type=F32)
                         + bg2_ref[...]) / GATE_NORMALIZER
    r_i = lax.broadcasted_iota(I32, (blk, blk), 0)
    c_i = lax.broadcasted_iota(I32, (blk, blk), 1)
    causal = c_i <= r_i
    tril = causal.astype(BF16)
    hi, mid, lo = _split3(log_a)
    tri_dot = lambda part: jnp.dot(tril, part, preferred_element_type=F32)
    g_all = tri_dot(hi) + tri_dot(mid) + tri_dot(lo)

    for h in range(H_A):
        kc = slice(h * DK_A, (h + 1) * DK_A)
        vc = slice(h * DV_A, (h + 1) * DV_A)
        g_cum = g_all[:, kc]
        g_last = g_cum[blk - 1:blk, :]
        k = k_ref[:, kc]
        v = v_ref[:, vc]
        s_prev = s_scr[h]
        q_dec = q_ref[:, kc] * (DK_A ** -0.5) * jnp.exp(g_cum)
        k_dec = k * jnp.exp(-g_cum)
        att = lax.dot_general(q_dec, k_dec, (((1,), (1,)), ((), ())), preferred_element_type=F32)
        att = jnp.where(causal, att, 0.0)
        o = (jnp.dot(att, v, preferred_element_type=F32)
             + jnp.dot(q_dec, s_prev, preferred_element_type=F32))
        k_end = k * jnp.exp(g_last - g_cum)
        decay_col = jnp.transpose(jnp.broadcast_to(jnp.exp(g_last), (LANES, DK_A)))[:, 0:1]
        s_new = s_prev * decay_col + lax.dot_general(
            k_end, v, (((0,), (0,)), ((), ())), preferred_element_type=F32)
        s_scr[h] = s_new
        ms = jnp.mean(o * o, axis=-1, keepdims=True)
        r = r_ref[:, vc]
        o_ref[:, vc] = o * lax.rsqrt(ms + EPS) * on_ref[...] * (r * jax.nn.sigmoid(r))

    @pl.when(c == n_chunks - 1)
    def _():
        sout_ref[...] = s_scr[...]


def _gla_scan(qkv, gl, r, layer, w_g2, b_g2, onorm, s0, o_buf, *, row0, nb, t_len, name):
    blk = min(CHUNK, t_len)
    n_chunks = t_len // blk
    rb0 = row0 // blk
    has_s0 = s0 is not None
    rows = lambda b, c: rb0 + b * n_chunks + c
    in_specs = [
        pl.BlockSpec((blk, DK_TOT), lambda b, c: (rows(b, c), 0)),
        pl.BlockSpec((blk, DK_TOT), lambda b, c: (rows(b, c), 1)),
        pl.BlockSpec((blk, DV_TOT), lambda b, c: (rows(b, c), 2 * DK_TOT // DV_TOT)),
        pl.BlockSpec((blk, GATE_RANK), lambda b, c: (rows(b, c), 0)),
        pl.BlockSpec((None, GATE_RANK, DK_TOT), lambda b, c: (layer, 0, 0)),
        pl.BlockSpec((None, 1, DK_TOT), lambda b, c: (layer, 0, 0)),
        pl.BlockSpec((blk, DV_TOT), lambda b, c: (rows(b, c), 0)),
        pl.BlockSpec((None, 1, DV_A), lambda b, c: (layer, 0, 0)),
        pl.BlockSpec(memory_space=pl.ANY),
    ]
    args = [qkv, qkv, qkv, gl, w_g2, b_g2.reshape(N_A, 1, DK_TOT), r,
            onorm.reshape(N_A, 1, DV_A), o_buf]
    if has_s0:
        in_specs.append(pl.BlockSpec((None, None, H_A, DK_A, DV_A),
                                     lambda b, c: (layer, b, 0, 0, 0)))
        args.append(s0)
    return pl.pallas_call(
        functools.partial(_gla_kernel, blk=blk, n_chunks=n_chunks, has_s0=has_s0),
        out_shape=(jax.ShapeDtypeStruct(o_buf.shape, F32),
                   jax.ShapeDtypeStruct((nb, H_A, DK_A, DV_A), F32)),
        grid=(nb, n_chunks),
        in_specs=in_specs,
        out_specs=(pl.BlockSpec((blk, DV_TOT), lambda b, c: (rows(b, c), 0)),
                   pl.BlockSpec((None, H_A, DK_A, DV_A), lambda b, c: (b, 0, 0, 0))),
        scratch_shapes=[pltpu.VMEM((H_A, DK_A, DV_A), F32)],
        input_output_aliases={8: 0},
        compiler_params=_params(("arbitrary", "arbitrary")),
        name=name,
    )(*args)


def _cumsum_kernel(x_ref, o_ref, carry_ref, *, tb):
    @pl.when(pl.program_id(1) == 0)
    def _():
        carry_ref[...] = jnp.zeros_like(carry_ref)

    r_i = lax.broadcasted_iota(I32, (tb, tb), 0)
    c_i = lax.broadcasted_iota(I32, (tb, tb), 1)
    triu = (r_i <= c_i).astype(BF16)
    hi, mid, lo = _split3(x_ref[...])
    tri_dot = lambda part: jnp.dot(part, triu, preferred_element_type=F32)
    y = tri_dot(hi) + tri_dot(mid) + tri_dot(lo) + carry_ref[:, 0:1]
    o_ref[...] = y
    carry_ref[...] = jnp.broadcast_to(y[:, tb - 1:tb], carry_ref.shape)


def _cumsum_lanes(x, tb, name):
    nb, nh, t_len = x.shape
    spec = pl.BlockSpec((None, nh, tb), lambda b, j: (b, 0, j))
    return pl.pallas_call(
        functools.partial(_cumsum_kernel, tb=tb),
        out_shape=jax.ShapeDtypeStruct(x.shape, F32),
        grid=(nb, t_len // tb),
        in_specs=[spec],
        out_specs=spec,
        scratch_shapes=[pltpu.VMEM((nh, LANES), F32)],
        compiler_params=_params(("arbitrary", "arbitrary")),
        name=name,
    )(x)


ATT_TILE = 256
KV_TILE = 512


def _fox_prompt_kernel(q_ref, k_ref, v_ref, cq_ref, ck_ref, _obuf, o_ref, m_ref, l_ref, acc_ref,
                       *, n_k):
    qi = pl.program_id(1)
    ki = pl.program_id(2)
    last_k = (qi * ATT_TILE) // KV_TILE

    @pl.when(ki == 0)
    def _():
        m_ref[...] = jnp.full_like(m_ref, -jnp.inf)
        l_ref[...] = jnp.zeros_like(l_ref)
        acc_ref[...] = jnp.zeros_like(acc_ref)

    def accumulate(masked):
        if masked:
            qpos = qi * ATT_TILE + lax.broadcasted_iota(I32, (ATT_TILE, KV_TILE), 0)
            kpos = ki * KV_TILE + lax.broadcasted_iota(I32, (ATT_TILE, KV_TILE), 1)
            visible = kpos <= qpos
        for h in range(H_B):
            cols = slice(h * HD_B, (h + 1) * HD_B)
            s = lax.dot_general(q_ref[:, cols], k_ref[:, cols], (((1,), (1,)), ((), ())),
                                preferred_element_type=F32) * (HD_B ** -0.5)
            s = s + cq_ref[:, h:h + 1] - ck_ref[h:h + 1, :]
            if masked:
                s = jnp.where(visible, s, -jnp.inf)
            m_prev = m_ref[h]
            m_next = jnp.maximum(m_prev, jnp.max(s, axis=1, keepdims=True))
            alpha = jnp.exp(m_prev - m_next)
            p = jnp.exp(s - jnp.concatenate([m_next] * (KV_TILE // LANES), axis=1))
            l_ref[h] = alpha * l_ref[h] + jnp.sum(p, axis=1, keepdims=True)
            acc_ref[:, cols] = acc_ref[:, cols] * alpha + jnp.dot(
                p, v_ref[:, cols], preferred_element_type=F32)
            m_ref[h] = m_next

    pl.when(ki < last_k)(functools.partial(accumulate, False))
    pl.when(ki == last_k)(functools.partial(accumulate, True))

    @pl.when(ki == n_k - 1)
    def _():
        for h in range(H_B):
            cols = slice(h * HD_B, (h + 1) * HD_B)
            o_ref[:, cols] = acc_ref[:, cols] / l_ref[h]


def _fox_prompt(q, k, v, cq, ck_t, o_buf, geo, name):
    n_q = geo.tp // ATT_TILE
    n_k = geo.tp // KV_TILE
    used_k = lambda qi, ki: jnp.minimum(ki, (qi * ATT_TILE) // KV_TILE)
    kv_blk = lambda b, qi, ki: (b * n_k + used_k(qi, ki), 0)
    q_blk = lambda b, qi, ki: (b * n_q + qi, 0)
    return pl.pallas_call(
        functools.partial(_fox_prompt_kernel, n_k=n_k),
        out_shape=jax.ShapeDtypeStruct(o_buf.shape, F32),
        grid=(geo.bp, n_q, n_k),
        in_specs=[pl.BlockSpec((ATT_TILE, D_MODEL), q_blk),
                  pl.BlockSpec((KV_TILE, D_MODEL), kv_blk),
                  pl.BlockSpec((KV_TILE, D_MODEL), kv_blk),
                  pl.BlockSpec((ATT_TILE, H_B), q_blk),
                  pl.BlockSpec((None, H_B, KV_TILE),
                               lambda b, qi, ki: (b, 0, used_k(qi, ki))),
                  pl.BlockSpec(memory_space=pl.ANY)],
        out_specs=pl.BlockSpec((ATT_TILE, D_MODEL), q_blk),
        scratch_shapes=[pltpu.VMEM((H_B, ATT_TILE, LANES), F32),
                        pltpu.VMEM((H_B, ATT_TILE, LANES), F32),
                        pltpu.VMEM((ATT_TILE, D_MODEL), F32)],
        input_output_aliases={5: 0},
        compiler_params=_params(("arbitrary", "arbitrary", "arbitrary")),
        name=name,
    )(q, k, v, cq, ck_t, o_buf)


def _fox_sample_kernel(q_ref, kp_ref, vp_ref, kn_ref, vn_ref, cq_ref, ckp_ref, ckn_ref, _oprev,
                       o_ref):
    t_len = q_ref.shape[0]
    scale = HD_B ** -0.5
    nt_dims = (((1,), (1,)), ((), ()))
    r_i = lax.broadcasted_iota(I32, (t_len, t_len), 0)
    c_i = lax.broadcasted_iota(I32, (t_len, t_len), 1)
    causal = c_i <= r_i
    for h in range(H_B):
        cols = slice(h * HD_B, (h + 1) * HD_B)
        q = q_ref[:, cols]
        cq = cq_ref[:, h:h + 1]
        n_past = kp_ref.shape[0] // H_B
        k_past = kp_ref[pl.ds(h, n_past, stride=H_B), :]
        v_past = vp_ref[pl.ds(h, n_past, stride=H_B), :]
        s_past = lax.dot_general(q, k_past, nt_dims, preferred_element_type=F32) * scale
        s_past = s_past + cq - ckp_ref[h:h + 1, :]
        s_new = lax.dot_general(q, kn_ref[:, cols], nt_dims, preferred_element_type=F32) * scale
        s_new = jnp.where(causal, s_new + cq - ckn_ref[h:h + 1, :], -jnp.inf)
        m = jnp.maximum(jnp.max(s_past, axis=1, keepdims=True),
                        jnp.max(s_new, axis=1, keepdims=True))
        p_past = jnp.exp(s_past - m)
        p_new = jnp.exp(s_new - m)
        denom = jnp.sum(p_past, axis=1, keepdims=True) + jnp.sum(p_new, axis=1, keepdims=True)
        o = (jnp.dot(p_past, v_past, preferred_element_type=F32)
             + jnp.dot(p_new, vn_ref[:, cols], preferred_element_type=F32))
        o_ref[:, cols] = o / denom


def _fox_sample(q, cache_k, cache_v, k_new, v_new, cq, ck_past, ck_new, o_prev, geo, name):
    rb0 = _n_prompt(geo) // geo.ts
    new_blk = pl.BlockSpec((geo.ts, D_MODEL), lambda b: (rb0 + b, 0))
    past_blk = pl.BlockSpec((None, geo.past * H_B, HD_B), lambda b: (b, 0, 0))
    return pl.pallas_call(
        _fox_sample_kernel,
        out_shape=jax.ShapeDtypeStruct(o_prev.shape, F32),
        grid=(geo.bs,),
        in_specs=[new_blk, past_blk, past_blk, new_blk, new_blk,
                  pl.BlockSpec((None, geo.ts, H_B), lambda b: (b, 0, 0)),
                  pl.BlockSpec((None, H_B, geo.past), lambda b: (b, 0, 0)),
                  pl.BlockSpec((None, H_B, geo.ts), lambda b: (b, 0, 0)),
                  pl.BlockSpec(memory_space=pl.ANY)],
        out_specs=new_blk,
        input_output_aliases={8: 0},
        compiler_params=_params(("arbitrary",)),
        name=name,
    )(q, cache_k, cache_v, k_new, v_new, cq, ck_past, ck_new, o_prev)


def _max_tiles(n_rows):
    return -(-(n_rows * TOP_K + N_EXPERTS * (ROW_TILE - 1)) // ROW_TILE)


def _route_plan(top_i, rank, counts, n_rows):
    counts = counts.reshape(N_EXPERTS).astype(I32)
    padded = ((counts + ROW_TILE - 1) // ROW_TILE) * ROW_TILE
    g_end = jnp.cumsum(padded)
    g_start = g_end - padded
    onehot = top_i[..., None] == jnp.arange(N_EXPERTS, dtype=I32)
    pos = jnp.sum(jnp.where(onehot, g_start, 0), axis=-1) + rank
    n_tiles = g_end[-1] // ROW_TILE
    max_tiles = _max_tiles(n_rows)
    tile_ids = jnp.arange(max_tiles, dtype=I32)
    tile_e = jnp.sum(((tile_ids * ROW_TILE)[:, None] >= g_end[None, :]).astype(I32), axis=1)
    last_e = jnp.sum((((n_tiles - 1) * ROW_TILE) >= g_end).astype(I32))
    tile_e = jnp.where(tile_ids < n_tiles, tile_e, last_e).astype(I32)
    first = (tile_ids == 0) | (tile_e != jnp.roll(tile_e, 1))
    end_tile = jnp.sum(jnp.where(tile_e[:, None] == jnp.arange(N_EXPERTS, dtype=I32), g_end, 0),
                       axis=1) // ROW_TILE
    next_e = jnp.where(end_tile < n_tiles, tile_e[jnp.minimum(end_tile, max_tiles - 1)], -1)
    tiles = (tile_e, n_tiles.reshape(1).astype(I32), first.astype(I32), next_e.astype(I32))
    return (pos.reshape(-1).astype(I32), tiles,
            (g_start + counts).astype(I32), (padded - counts).astype(I32))


DRAIN_UNROLL = 16
DMA_THREADS = 2


def _token_slab(ref, token):
    row0 = token * SLAB_ROWS
    if not isinstance(row0, int):
        row0 = pl.multiple_of(row0, SLAB_ROWS)
    if len(ref.shape) == 2:
        return ref.at[pl.ds(row0, SLAB_ROWS), :]
    return ref.at[:, pl.ds(row0, SLAB_ROWS), :]


def _token_copy(src, src_token, dst, dst_token, sem):
    return pltpu.make_async_copy(_token_slab(src, src_token), _token_slab(dst, dst_token), sem)


def _dispatch_kernel(pos_ref, pad0_ref, padn_ref, h_ref, xs_ref, zero_ref, sem):
    i = pl.program_id(0)

    @pl.when(i == 0)
    def _():
        zero_ref[...] = jnp.zeros_like(zero_ref)

        def per_expert(e, carry):
            start = pad0_ref[e]
            count = padn_ref[e]

            def issue(j, c):
                _token_copy(zero_ref, 0, xs_ref, start + j, sem).start()
                return c

            def drain(j, c):
                _token_copy(zero_ref, 0, xs_ref, start + j, sem).wait()
                return c

            lax.fori_loop(0, count, issue, 0)
            lax.fori_loop(0, count, drain, 0)
            return carry

        lax.fori_loop(0, N_EXPERTS, per_expert, 0)

    def issue_row(r, c):
        for k in range(TOP_K):
            p = pos_ref[(i * ROW_TILE + r) * TOP_K + k]
            _token_copy(h_ref, r, xs_ref, p, sem).start(priority=k % DMA_THREADS)
        return c

    def drain_rows(g, c):
        for _ in range(DRAIN_UNROLL * TOP_K):
            _token_copy(h_ref, 0, xs_ref, 0, sem).wait()
        return c

    lax.fori_loop(0, ROW_TILE, issue_row, 0)
    lax.fori_loop(0, ROW_TILE // DRAIN_UNROLL, drain_rows, 0)


def _dispatch(h_tm, pos, pad0, padn, layer):
    n = h_tm.shape[0] // SLAB_ROWS
    rows_sorted = _max_tiles(n) * ROW_TILE
    grid_spec = pltpu.PrefetchScalarGridSpec(
        num_scalar_prefetch=3,
        grid=(n // ROW_TILE,),
        in_specs=[pl.BlockSpec((ROW_TILE * SLAB_ROWS, LANES), lambda i, *_: (i, 0))],
        out_specs=pl.BlockSpec(memory_space=pl.ANY),
        scratch_shapes=[pltpu.VMEM((SLAB_ROWS, LANES), U32), pltpu.SemaphoreType.DMA],
    )
    return pl.pallas_call(
        _dispatch_kernel,
        out_shape=jax.ShapeDtypeStruct((rows_sorted * SLAB_ROWS, LANES), U32),
        grid_spec=grid_spec,
        compiler_params=_params(("arbitrary",)),
        name=f"moe_dispatch_{layer}",
    )(pos, pad0, padn, h_tm)


FF_TILE = 1024


W_CHUNK = 256


def _expert_weights_step(fetch, te_ref, first_ref, next_ref, stage_ref, cache_ref, n_pass):
    n = pl.program_id(0)
    r = pl.program_id(1)

    @pl.when(first_ref[r] == 1)
    def _():
        @pl.when((n == 0) & (r == 0))
        def _():
            for copy in fetch(te_ref[0], 0):
                copy.start()

        for copy in fetch(te_ref[r], n):
            copy.wait()
        for part in range(stage_ref.shape[0]):
            for c in range(stage_ref.shape[1] // W_CHUNK):
                rows = pl.ds(c * W_CHUNK, W_CHUNK)
                cache_ref[part, rows, :] = stage_ref[part, rows, :].astype(BF16)
        nxt = next_ref[r]

        @pl.when(nxt >= 0)
        def _():
            for copy in fetch(nxt, n):
                copy.start()

        @pl.when((nxt < 0) & (n + 1 < n_pass))
        def _():
            for copy in fetch(te_ref[0], n + 1):
                copy.start()


def _moe_up_kernel(te_ref, nt_ref, first_ref, next_ref, xs_ref, w_ref, bg_ref, bu_ref, a_ref,
                   stage_ref, cache_ref, sems, *, layer):
    def fetch(e, n):
        col0 = n * FF_TILE
        if not isinstance(col0, int):
            col0 = pl.multiple_of(col0, FF_TILE)
        return [pltpu.make_async_copy(
            w_ref.at[layer, e, :, pl.ds(col0 + part * D_FF, FF_TILE)], stage_ref.at[part],
            sems.at[part]) for part in range(2)]

    @pl.when(pl.program_id(1) < nt_ref[0])
    def _():
        _expert_weights_step(fetch, te_ref, first_ref, next_ref, stage_ref, cache_ref,
                             D_FF // FF_TILE)
        x = _load_packed_token_major(xs_ref, ROW_TILE)
        g = jnp.dot(x, cache_ref[0], preferred_element_type=F32) + bg_ref[...]
        u = jnp.dot(x, cache_ref[1], preferred_element_type=F32) + bu_ref[...]
        g = jnp.minimum(g, SWIGLU_LIMIT)
        u = jnp.clip(u, -SWIGLU_LIMIT, SWIGLU_LIMIT)
        a_ref[...] = (g * jax.nn.sigmoid(SWIGLU_ALPHA * g) * (u + 1.0)).astype(BF16)


def _moe_down_kernel(te_ref, nt_ref, first_ref, next_ref, a_ref, w_ref, b_ref, y_ref,
                     stage_ref, cache_ref, sems, *, layer):
    def fetch(e, n):
        return [pltpu.make_async_copy(w_ref.at[layer, e], stage_ref.at[0], sems.at[0])]

    @pl.when(pl.program_id(1) < nt_ref[0])
    def _():
        _expert_weights_step(fetch, te_ref, first_ref, next_ref, stage_ref, cache_ref, 1)
        y = jnp.dot(a_ref[...], cache_ref[0], preferred_element_type=F32) + b_ref[...]
        _store_token_major(y_ref, y)


def _moe_experts(xs_tm, tiles, w_in, b_in, w_out, b_out, layer):
    rows_sorted = xs_tm.shape[0] // SLAB_ROWS
    max_tiles = rows_sorted // ROW_TILE
    u_blk0 = D_FF // FF_TILE
    used = lambda r, nt: jnp.minimum(r, nt[0] - 1)
    tm_spec = pl.BlockSpec(_tm_shape(ROW_TILE), lambda n, r, te, nt, fi, nx: (0, used(r, nt), 0))
    up_spec = pltpu.PrefetchScalarGridSpec(
        num_scalar_prefetch=4,
        grid=(D_FF // FF_TILE, max_tiles),
        in_specs=[
            pl.BlockSpec((ROW_TILE * SLAB_ROWS, LANES),
                         lambda n, r, te, nt, fi, nx: (used(r, nt), 0)),
            pl.BlockSpec(memory_space=pl.ANY),
            pl.BlockSpec((None, None, 1, FF_TILE),
                         lambda n, r, te, nt, fi, nx: (layer, te[r], 0, n)),
            pl.BlockSpec((None, None, 1, FF_TILE),
                         lambda n, r, te, nt, fi, nx: (layer, te[r], 0, u_blk0 + n)),
        ],
        out_specs=pl.BlockSpec((ROW_TILE, FF_TILE),
                               lambda n, r, te, nt, fi, nx: (used(r, nt), n)),
        scratch_shapes=[pltpu.VMEM((2, D_MODEL, FF_TILE), F32),
                        pltpu.VMEM((2, D_MODEL, FF_TILE), BF16),
                        pltpu.SemaphoreType.DMA((2,))],
    )
    b_in4 = b_in.reshape(DEPTH, N_EXPERTS, 1, 2 * D_FF)
    act = pl.pallas_call(
        functools.partial(_moe_up_kernel, layer=layer),
        out_shape=jax.ShapeDtypeStruct((rows_sorted, D_FF), BF16),
        grid_spec=up_spec,
        compiler_params=_params(("arbitrary", "arbitrary")),
        name=f"moe_up_{layer}",
    )(*tiles, xs_tm, w_in, b_in4, b_in4)
    down_spec = pltpu.PrefetchScalarGridSpec(
        num_scalar_prefetch=4,
        grid=(1, max_tiles),
        in_specs=[
            pl.BlockSpec((ROW_TILE, D_FF), lambda n, r, te, nt, fi, nx: (used(r, nt), 0)),
            pl.BlockSpec(memory_space=pl.ANY),
            pl.BlockSpec((None, None, 1, D_MODEL),
                         lambda n, r, te, nt, fi, nx: (layer, te[r], 0, 0)),
        ],
        out_specs=tm_spec,
        scratch_shapes=[pltpu.VMEM((1, D_FF, D_MODEL), F32),
                        pltpu.VMEM((1, D_FF, D_MODEL), BF16),
                        pltpu.SemaphoreType.DMA((1,))],
    )
    return pl.pallas_call(
        functools.partial(_moe_down_kernel, layer=layer),
        out_shape=jax.ShapeDtypeStruct(_tm_shape(rows_sorted), F32),
        grid_spec=down_spec,
        compiler_params=_params(("arbitrary", "arbitrary")),
        name=f"moe_down_{layer}",
    )(*tiles, act, w_out, b_out.reshape(DEPTH, N_EXPERTS, 1, D_MODEL))


def _combine_kernel(pos_ref, ys_ref, tw_ref, x_ref, gp_ref, gs_ref, o_ref, buf_ref, wexp_ref, sem,
                    *, n_prompt_tiles):
    i = pl.program_id(0)

    def issue_row(r, c):
        for k in range(TOP_K):
            p = pos_ref[(i * ROW_TILE + r) * TOP_K + k]
            _token_copy(ys_ref, p, buf_ref.at[k], r, sem).start(priority=k % DMA_THREADS)
        return c

    def drain_rows(g, c):
        for _ in range(DRAIN_UNROLL * TOP_K):
            _token_copy(ys_ref, 0, buf_ref.at[0], 0, sem).wait()
        return c

    lax.fori_loop(0, ROW_TILE, issue_row, 0)
    lax.fori_loop(0, ROW_TILE // DRAIN_UNROLL, drain_rows, 0)
    tw = tw_ref[...]
    for j in range(SLAB_ROWS):
        wexp_ref[pl.ds(j, ROW_TILE, stride=SLAB_ROWS), :] = tw
    for c in range(SLAB_ROWS):
        rows = pl.ds(c * ROW_TILE, ROW_TILE)
        w_cols = [wexp_ref[rows, k:k + 1] for k in range(TOP_K)]
        for p in range(SLAB_PARTS):
            y_tm = w_cols[0] * buf_ref[0, p, rows, :]
            for k in range(1, TOP_K):
                y_tm = y_tm + w_cols[k] * buf_ref[k, p, rows, :]
            buf_ref[0, p, rows, :] = y_tm
    y = _load_token_major(buf_ref.at[0], ROW_TILE)
    gate = jnp.where(i < n_prompt_tiles, gp_ref[...], gs_ref[...])
    o_ref[...] = x_ref[...] + gate * y


def _combine(ys, pos, top_w, x, gate_tabs, gate_comp, geo, layer):
    n = x.shape[0]
    gp, gs = _mod_specs(gate_comp, geo, lambda i, *_: i, lambda i, *_: 0, D_MODEL)
    x_spec = pl.BlockSpec((ROW_TILE, D_MODEL), lambda i, *_: (i, 0))
    grid_spec = pltpu.PrefetchScalarGridSpec(
        num_scalar_prefetch=1,
        grid=(n // ROW_TILE,),
        in_specs=[pl.BlockSpec(memory_space=pl.ANY),
                  pl.BlockSpec((ROW_TILE, LANES), lambda i, *_: (i, 0)),
                  x_spec, gp, gs],
        out_specs=x_spec,
        scratch_shapes=[pltpu.VMEM((TOP_K,) + _tm_shape(ROW_TILE), F32),
                        pltpu.VMEM((ROW_TILE * SLAB_ROWS, LANES), F32),
                        pltpu.SemaphoreType.DMA],
    )
    return pl.pallas_call(
        functools.partial(_combine_kernel, n_prompt_tiles=_n_prompt(geo) // ROW_TILE),
        out_shape=jax.ShapeDtypeStruct((n, D_MODEL), F32),
        grid_spec=grid_spec,
        compiler_params=_params(("arbitrary",)),
        name=f"moe_combine_{layer}",
    )(pos, ys, top_w, x, gate_tabs[0], gate_tabs[1])


def _trunk(x, cs, state_gla, cache_k, cache_v, cache_logf, geo,
           w_mod, b_mod, norm_mix, norm_ffn,
           w_gla_in, w_gla_g2, b_gla_g2, gla_onorm, w_gla_out,
           w_kv_mod, b_kv_mod, norm_kv, w_kv, b_f, k_norm,
           w_fox_q, q_norm, w_fox_out,
           w_router, b_router, w_moe_in, b_moe_in, w_moe_out, b_moe_out):
    n = _n_rows(geo)
    n_p = _n_prompt(geo)
    b_mod3 = b_mod.reshape(DEPTH, 1, 6 * D_MODEL)
    gla_p, gla_s = [], []
    kv = None
    for layer in range(DEPTH):
        if layer == N_A:
            kv_mod = _mm(cs, w_kv_mod, ncols=2 * D_MODEL, tn=1024, tm=MOD_ROWS, epi="bias",
                         bias=b_kv_mod.reshape(1, 1, 2 * D_MODEL), name="kv_mod")
            kv_tabs = _mod_tables(kv_mod, 2, geo)
            hk = _norm_mod(x, norm_kv, kv_tabs, 0, 1, geo, "norm_kv")
            k_new = _mm(hk, w_kv, ncols=D_MODEL, tn=1024, epi="headnorm", gain=k_norm, name="kv_k")
            v_new = _mm(hk, w_kv, ncols=D_MODEL, tn=1024, col0=D_MODEL, name="kv_v")
            logf = _mm(hk, w_kv[:, 2 * D_MODEL:], ncols=H_B, tn=H_B, epi="logsig",
                       bias=b_f.reshape(1, 1, H_B), name="kv_f")
            logf_p = logf[:n_p].reshape(geo.bp, geo.tp, H_B)
            logf_s = logf[n_p:].reshape(geo.bs, geo.ts, H_B)
            c_p_t = _cumsum_lanes(logf_p.transpose(0, 2, 1), 512, "cumsum_p")
            cq_p = c_p_t.transpose(0, 2, 1).reshape(n_p, H_B)
            all_s = jnp.concatenate([cache_logf.astype(F32), logf_s], axis=1).transpose(0, 2, 1)
            c_s_t = _cumsum_lanes(all_s, geo.past + geo.ts, "cumsum_s")
            ck_past = c_s_t[:, :, :geo.past]
            ck_new = c_s_t[:, :, geo.past:]
            cq_s = ck_new.transpose(0, 2, 1)
            cache_k2 = cache_k.reshape(geo.bs, geo.past * H_B, HD_B)
            cache_v2 = cache_v.reshape(geo.bs, geo.past * H_B, HD_B)
            kv = (k_new, v_new, logf_p, logf_s)

        mod = _mm(cs, w_mod, wl=layer, ncols=6 * D_MODEL, tn=1024, tm=MOD_ROWS, epi="bias",
                  bias=b_mod3, bl=layer, name=f"mod_{layer}")
        tabs = _mod_tables(mod, 6, geo)
        h = _norm_mod(x, norm_mix[layer], tabs, 0, 1, geo, f"norm_mix_{layer}")
        if layer < N_A:
            qkv = _mm(h, w_gla_in, wl=layer, ncols=2 * DK_TOT + DV_TOT, tn=1024, name=f"gla_qkv_{layer}")
            gcol = 2 * DK_TOT + DV_TOT
            gl = _mm(h, w_gla_in[layer, :, gcol:gcol + GATE_RANK], ncols=GATE_RANK, tn=GATE_RANK,
                     name=f"gla_gl_{layer}")
            r = _mm(h, w_gla_in[layer, :, gcol + GATE_RANK:], ncols=DV_TOT, tn=1024,
                    name=f"gla_r_{layer}")
            o, s_p = _gla_scan(qkv, gl, r, layer, w_gla_g2, b_gla_g2, gla_onorm, None, h,
                               row0=0, nb=geo.bp, t_len=geo.tp, name=f"gla_scan_p_{layer}")
            o, s_s = _gla_scan(qkv, gl, r, layer, w_gla_g2, b_gla_g2, gla_onorm, state_gla, o,
                               row0=n_p, nb=geo.bs, t_len=geo.ts, name=f"gla_scan_s_{layer}")
            gla_p.append(s_p)
            gla_s.append(s_s)
            x = _mm(o, w_gla_out, wl=layer, ncols=D_MODEL, tn=1024, epi="resgate", xres=x,
                    gate_tabs=tabs, gate_comp=2, geo=geo, name=f"gla_out_{layer}")
        else:
            j = layer - N_A
            q = _mm(h, w_fox_q, wl=j, ncols=D_MODEL, tn=1024, epi="headnorm", gain=q_norm[j],
                    name=f"fox_q_{j}")
            o = _fox_prompt(q, kv[0], kv[1], cq_p, c_p_t, h, geo, f"fox_attn_p_{j}")
            o = _fox_sample(q, cache_k2, cache_v2, kv[0], kv[1], cq_s, ck_past, ck_new, o, geo,
                            f"fox_attn_s_{j}")
            x = _mm(o, w_fox_out, wl=j, ncols=D_MODEL, tn=1024, epi="resgate", xres=x,
                    gate_tabs=tabs, gate_comp=2, geo=geo, name=f"fox_out_{j}")

        h2, top_i, top_w, rank, counts = _norm_router(x, norm_ffn[layer], tabs, 3, 4,
                                                      w_router, b_router, layer, geo)
        pos, tiles, pad0, padn = _route_plan(top_i[:, :TOP_K], rank[:, :TOP_K], counts, n)
        xs = _dispatch(h2, pos, pad0, padn, layer)
        ys = _moe_experts(xs, tiles, w_moe_in, b_moe_in, w_moe_out, b_moe_out, layer)
        x = _combine(ys, pos, top_w, x, tabs, 5, geo, layer)
    return x, gla_p, gla_s, kv


def kernel(x_prompt, x_sample, c_prompt, c_sample, state_gla, cache_k, cache_v, cache_logf, w_mod, b_mod, norm_mix, norm_ffn, w_gla_in, w_gla_g2, b_gla_g2, gla_onorm, w_gla_out, w_kv_mod, b_kv_mod, norm_kv, w_kv, b_f, k_norm, w_fox_q, q_norm, w_fox_out, w_router, b_router, w_moe_in, b_moe_in, w_moe_out, b_moe_out):
    bp, tp, _ = x_prompt.shape
    bs, ts, _ = x_sample.shape
    geo = Geo(bp, tp, bs, ts, cache_k.shape[1])
    assert bs * ts == ROW_TILE and tp % ROW_TILE == 0 and tp % 512 == 0 and bp + bs <= MOD_ROWS
    n_p = bp * tp
    x = jnp.concatenate([x_prompt.reshape(n_p, D_MODEL), x_sample.reshape(bs * ts, D_MODEL)], axis=0)
    c = jnp.concatenate([c_prompt, c_sample], axis=0).astype(F32)
    cs = jnp.pad(c * jax.nn.sigmoid(c), ((0, MOD_ROWS - bp - bs), (0, 0)))
    x, gla_p, gla_s, (k_new, v_new, logf_p, logf_s) = _trunk(
        x, cs, state_gla, cache_k, cache_v, cache_logf, geo,
        w_mod, b_mod, norm_mix, norm_ffn,
        w_gla_in, w_gla_g2, b_gla_g2, gla_onorm, w_gla_out,
        w_kv_mod, b_kv_mod, norm_kv, w_kv, b_f, k_norm,
        w_fox_q, q_norm, w_fox_out,
        w_router, b_router, w_moe_in, b_moe_in, w_moe_out, b_moe_out)
    return (x[:n_p].reshape(bp, tp, D_MODEL),
            x[n_p:].reshape(bs, ts, D_MODEL),
            jnp.stack(gla_p), jnp.stack(gla_s),
            k_new[:n_p].reshape(bp, tp, H_B, HD_B),
            v_new[:n_p].reshape(bp, tp, H_B, HD_B),
            logf_p,
            k_new[n_p:].reshape(bs, ts, H_B, HD_B),
            v_new[n_p:].reshape(bs, ts, H_B, HD_B),
            logf_s)
```
